```python
import jax
import jax.numpy as jnp
from jax import lax
import numpy as np

D_MODEL = 1024
BATCH = 8
SEQ = 2048
DEPTH = 4
DEC_BATCH = 16
DEC_SEQ = 64
PAST_LEN = 4096

CHUNK = 64
SUB_CHUNK = 16
N_SUB = CHUNK // SUB_CHUNK
N_EVEN = (DEPTH + 1) // 2
N_ODD = DEPTH // 2

RWKV_HEADS = 8
RWKV_HEAD_DIM = 64
RWKV_DIM = RWKV_HEADS * RWKV_HEAD_DIM
DECAY_LORA = 64
AAA_LORA = 64
GATE_LORA = 128
RWKV_PROJ = 3 * RWKV_DIM + DECAY_LORA + AAA_LORA + GATE_LORA
RWKV_SPLITS = (RWKV_DIM, 2 * RWKV_DIM, 3 * RWKV_DIM, 3 * RWKV_DIM + DECAY_LORA, 3 * RWKV_DIM + DECAY_LORA + AAA_LORA)
RWKV_GN_EPS = RWKV_HEAD_DIM * 1e-5
DECAY_SCALE = 0.6065306597126334

GLA_HEADS = 4
GLA_DK = 64
GLA_DV = 128
GLA_KDIM = GLA_HEADS * GLA_DK
GLA_VDIM = GLA_HEADS * GLA_DV
GLA_GATE_RANK = 16
GLA_TAU = 16.0
GLA_PROJ = 2 * GLA_KDIM + GLA_VDIM + GLA_GATE_RANK + GLA_VDIM
GLA_SPLITS = (GLA_KDIM, 2 * GLA_KDIM, 2 * GLA_KDIM + GLA_VDIM, 2 * GLA_KDIM + GLA_VDIM + GLA_GATE_RANK)

P_AB = RWKV_PROJ + GLA_PROJ
MIX_AB = RWKV_DIM + GLA_VDIM

HGRN_HEADS = 8
HGRN_HEAD_DIM = 128
HGRN_DIM = HGRN_HEADS * HGRN_HEAD_DIM
P_C = 4 * HGRN_DIM

D_FF = 2816
N_EXPERTS = 8
TOP_K = 2
EXPERT_D_FF = 2816
MOE_BLOCK = 128

ALPHA = (2 * DEPTH) ** 0.25
BETA = (8 * DEPTH) ** -0.25
NORM_EPS = 1e-5

kernel_name = 'hybrid_rwkv7_gla_hgrn2_moe_stream_step'


def layer_norm(x, g, b):
    x32 = x.astype(jnp.float32)
    xc = x32 - jnp.mean(x32, -1, keepdims=True)
    var = jnp.mean(xc * xc, -1, keepdims=True)
    return (xc * lax.rsqrt(var + NORM_EPS) * g + b).astype(x.dtype)


def rms_norm(x, w):
    x = x.astype(jnp.float32)
    return x * lax.rsqrt(jnp.mean(x * x, -1, keepdims=True) + NORM_EPS) * w


def swiglu(h, w1, w3, w2):
    return (jax.nn.silu(h @ w1) * (h @ w3)) @ w2


def chunk_gated_linear_attention(q, k, v, log_a, s0):
    bsz, t_len, n_h, dk = q.shape
    dv = v.shape[-1]
    n_chunks = -(-t_len // CHUNK)
    pad = n_chunks * CHUNK - t_len

    def to_chunks(a):
        a = jnp.pad(a.astype(jnp.float32), ((0, 0), (0, pad), (0, 0), (0, 0)))
        return a.reshape(bsz, n_chunks, CHUNK, n_h, a.shape[-1]).transpose(1, 0, 3, 2, 4)

    cross_mask = jnp.asarray((np.arange(CHUNK)[None, :] // SUB_CHUNK) < np.arange(N_SUB)[:, None])
    sub_causal = jnp.asarray(np.tril(np.ones((SUB_CHUNK, SUB_CHUNK), dtype=bool)))

    def body(s, blk):
        qc, kc, vc, gc = blk
        b = jnp.cumsum(gc, axis=2)
        o = jnp.einsum('bhld,bhdv->bhlv', qc * jnp.exp(b), s)
        qs = qc.reshape(bsz, n_h, N_SUB, SUB_CHUNK, dk)
        ks = kc.reshape(bsz, n_h, N_SUB, SUB_CHUNK, dk)
        vs = vc.reshape(bsz, n_h, N_SUB, SUB_CHUNK, dv)
        bs = b.reshape(bsz, n_h, N_SUB, SUB_CHUNK, dk)
        e = jnp.concatenate([jnp.zeros_like(bs[:, :, :1, 0]), bs[:, :, :-1, -1]], axis=2)
        qg = qs * jnp.exp(bs - e[:, :, :, None])
        kg = kc[:, :, None] * jnp.exp(jnp.minimum(e[:, :, :, None] - b[:, :, None], 0.0))
        sc = jnp.where(cross_mask[:, None, :], jnp.einsum('bhixd,bhisd->bhixs', qg, kg), 0.0)
        o_cross = jnp.einsum('bhixs,bhsv->bhixv', sc, vc)
        dec = jnp.where(sub_causal[:, :, None],
                        jnp.exp(jnp.minimum(bs[..., :, None, :] - bs[..., None, :, :], 0.0)), 0.0)
        sw = jnp.sum(qs[..., :, None, :] * ks[..., None, :, :] * dec, axis=-1)
        o_in = jnp.einsum('bhits,bhisv->bhitv', sw, vs)
        o = o + (o_cross + o_in).reshape(bsz, n_h, CHUNK, dv)
        b_last = b[:, :, -1]
        s = s * jnp.exp(b_last)[..., None] + jnp.einsum('bhld,bhlv->bhdv', kc * jnp.exp(b_last[:, :, None] - b), vc)
        return s, o

    s, o = lax.scan(body, s0.astype(jnp.float32), (to_chunks(q), to_chunks(k), to_chunks(v), to_chunks(log_a)))
    o = o.transpose(1, 0, 3, 2, 4).reshape(bsz, n_chunks * CHUNK, n_h, dv)[:, :t_len]
    return o, s


def rwkv7_recurrence(r, w, k, v, kk, a, s0):
    def step(s, inp):
        r_t, w_t, k_t, v_t, kk_t, a_t = inp
        sa = jnp.einsum('bhvk,bhk->bhv', s, kk_t)
        s = (s * w_t[:, :, None, :] - sa[..., None] * (kk_t * a_t)[:, :, None, :]
             + v_t[..., None] * k_t[:, :, None, :])
        return s, jnp.einsum('bhvk,bhk->bhv', s, r_t)
    seq = tuple(jnp.moveaxis(t, 1, 0) for t in (r, w, k, v, kk, a))
    s, o = lax.scan(step, s0.astype(jnp.float32), seq)
    return jnp.moveaxis(o, 0, 1), s


def mix_rwkv_gla(h, j, p, shift_prev, s_rwkv, s_gla):
    f32 = jnp.float32
    bsz, t_len, _ = h.shape
    proj = h @ p['w_in_ab'][j]
    pr, pg = proj[..., :RWKV_PROJ], proj[..., RWKV_PROJ:]
    prev = jnp.concatenate([shift_prev[:, None].astype(pr.dtype), pr[:, :-1]], axis=1)
    xs = (pr + (prev - pr) * p['rwkv_mu'][j]).astype(f32)
    r, k, v, wd, ad, gd = jnp.split(xs, list(RWKV_SPLITS), axis=-1)
    hd = (bsz, t_len, RWKV_HEADS, RWKV_HEAD_DIM)
    w = jnp.exp(-DECAY_SCALE * jax.nn.sigmoid(p['rwkv_w0'][j] + jnp.tanh(wd) @ p['rwkv_w_up'][j])).reshape(hd)
    a = jax.nn.sigmoid(p['rwkv_a0'][j] + ad @ p['rwkv_a_up'][j]).reshape(hd)
    g = jax.nn.sigmoid(gd) @ p['rwkv_g_up'][j]
    r, k, v = r.reshape(hd), k.reshape(hd), v.reshape(hd)
    kk = k * p['rwkv_k_k'][j].reshape(RWKV_HEADS, RWKV_HEAD_DIM)
    kk = kk / jnp.maximum(jnp.sqrt(jnp.sum(kk * kk, -1, keepdims=True)), 1e-12)
    k = k * (1.0 + (a - 1.0) * p['rwkv_k_a'][j].reshape(RWKV_HEADS, RWKV_HEAD_DIM))
    o, s_rwkv = rwkv7_recurrence(r, w, k, v, kk, a, s_rwkv)
    mu_o = jnp.mean(o, -1, keepdims=True)
    var_o = jnp.mean(jnp.square(o - mu_o), -1, keepdims=True)
    o = ((o - mu_o) * lax.rsqrt(var_o + RWKV_GN_EPS)).reshape(bsz, t_len, RWKV_DIM) * p['rwkv_gn_w'][j] + p['rwkv_gn_b'][j]
    bonus = jnp.sum(r * k * p['rwkv_r_k'][j], -1, keepdims=True) * v
    o_rwkv = (o + bonus.reshape(bsz, t_len, RWKV_DIM)) * g
    q, kg, vg, gz, og = jnp.split(pg.astype(f32), list(GLA_SPLITS), axis=-1)
    log_a = jax.nn.log_sigmoid(gz @ p['gla_gate_up'][j] + p['gla_gate_b'][j]) / GLA_TAU
    kd = (bsz, t_len, GLA_HEADS, GLA_DK)
    o_g, s_gla = chunk_gated_linear_attention(q.reshape(kd) * GLA_DK ** -0.5, kg.reshape(kd),
                                              vg.reshape(bsz, t_len, GLA_HEADS, GLA_DV), log_a.reshape(kd), s_gla)
    o_g = rms_norm(o_g, p['gla_norm_w'][j]).reshape(bsz, t_len, GLA_VDIM) * jax.nn.silu(og)
    out = jnp.concatenate([o_rwkv, o_g], axis=-1).astype(h.dtype) @ p['w_out_ab'][j]
    return out, pr[:, -1], s_rwkv, s_gla


def mix_hgrn2(h, j, lb, p, s0):
    f32 = jnp.float32
    bsz, t_len, _ = h.shape
    proj = (h @ p['w_in_c'][j]).astype(f32)
    q, f, i, g = jnp.split(proj, 4, axis=-1)
    lb = lb.astype(f32)
    log_f = jnp.logaddexp(jnp.log(lb), jnp.log1p(-lb) + jax.nn.log_sigmoid(f))
    k = (1.0 - lb) * jax.nn.sigmoid(-f)
    hd = (bsz, t_len, HGRN_HEADS, HGRN_HEAD_DIM)
    o, s = chunk_gated_linear_attention(jax.nn.silu(q).reshape(hd), k.reshape(hd), i.reshape(hd), log_f.reshape(hd), s0)
    o = rms_norm(o, p['hgrn_norm_w'][j]).reshape(bsz, t_len, HGRN_DIM) * jax.nn.silu(g)
    return o.astype(h.dtype) @ p['w_out_c'][j], s


def moe_swiglu(h, router, w1, w3, w2):
    f32 = jnp.float32
    bsz, t_len, d = h.shape
    x = h.reshape(-1, d)
    n_tok = x.shape[0]
    probs = jax.nn.softmax((x @ router).astype(f32), axis=-1)
    top_p, top_e = lax.top_k(probs, TOP_K)
    top_p = top_p / jnp.sum(top_p, -1, keepdims=True)
    n_slots = n_tok * TOP_K
    flat_e = top_e.reshape(-1)
    order = jnp.argsort(flat_e)
    sorted_e = flat_e[order]
    counts = jnp.bincount(flat_e, length=N_EXPERTS)
    padded = (counts + MOE_BLOCK - 1) // MOE_BLOCK * MOE_BLOCK
    group_start = jnp.cumsum(counts) - counts
    padded_end = jnp.cumsum(padded)
    padded_start = padded_end - padded
    dest = padded_start[sorted_e] + jnp.arange(n_slots) - group_start[sorted_e]
    n_blocks = -(-n_slots // MOE_BLOCK) + N_EXPERTS
    slot_token = order // TOP_K
    row_token = jnp.zeros((n_blocks * MOE_BLOCK,), jnp.int32).at[dest].set(slot_token.astype(jnp.int32))
    block_expert = jnp.minimum(jnp.searchsorted(padded_end, jnp.arange(n_blocks) * MOE_BLOCK, side='right'),
                               N_EXPERTS - 1)

    def expert_block(args):
        xb, e = args
        return (jax.nn.silu(xb @ w1[e]) * (xb @ w3[e])) @ w2[e]

    y_rows = lax.map(expert_block, (x[row_token].reshape(n_blocks, MOE_BLOCK, d), block_expert))
    y_slots = y_rows.reshape(-1, d)[dest].astype(f32) * top_p.reshape(-1)[order][:, None]
    y = jnp.zeros((n_tok, d), f32).at[slot_token].add(y_slots)
    return y.astype(h.dtype).reshape(bsz, t_len, d)


def run_trunk(x, s_rwkv, s_shift, s_gla, s_hgrn, lower_bounds, p):
    new_rwkv, new_shift, new_gla, new_hgrn = [], [], [], []
    for layer in range(DEPTH):
        j = layer // 2
        if layer % 2 == 0:
            out, sh, sr, sg = mix_rwkv_gla(x, j, p, s_shift[j], s_rwkv[j], s_gla[j])
            x = layer_norm(ALPHA * x + out, p['ln_ab_g'][j], p['ln_ab_b'][j])
            ffn = swiglu(x, p['ffn_w1'][j], p['ffn_w3'][j], p['ffn_w2'][j])
            x = layer_norm(ALPHA * x + ffn, p['ln_ffn_g'][j], p['ln_ffn_b'][j])
            new_rwkv.append(sr)
            new_shift.append(sh)
            new_gla.append(sg)
        else:
            out, sc = mix_hgrn2(x, j, lower_bounds[layer], p, s_hgrn[j])
            x = layer_norm(ALPHA * x + out, p['ln_c_g'][j], p['ln_c_b'][j])
            moe = moe_swiglu(x, p['moe_router'][j], p['moe_w1'][j], p['moe_w3'][j], p['moe_w2'][j])
            x = layer_norm(ALPHA * x + moe, p['ln_moe_g'][j], p['ln_moe_b'][j])
            new_hgrn.append(sc)
    return x, jnp.stack(new_rwkv), jnp.stack(new_shift), jnp.stack(new_gla), jnp.stack(new_hgrn)


def setup_inputs(seed: int = 0) -> dict:
    key = jax.random.key(seed)
    ks = iter(jax.random.split(key, 64))

    def nrm(shape, scale):
        return jax.random.normal(next(ks), shape, jnp.float32) * scale

    def gain(shape):
        return 1.0 + nrm(shape, 0.05)

    ne, no, ex = N_EVEN, N_ODD, N_EXPERTS
    return {
        'x_prompt': nrm((BATCH, SEQ, D_MODEL), 1.0),
        'x_sample': nrm((DEC_BATCH, DEC_SEQ, D_MODEL), 1.0),
        'state_rwkv': nrm((ne, DEC_BATCH, RWKV_HEADS, RWKV_HEAD_DIM, RWKV_HEAD_DIM), 0.3),
        'cache_rwkv_shift': nrm((ne, DEC_BATCH, RWKV_PROJ), 1.0),
        'state_gla': nrm((ne, DEC_BATCH, GLA_HEADS, GLA_DK, GLA_DV), 0.3),
        'state_hgrn': nrm((no, DEC_BATCH, HGRN_HEADS, HGRN_HEAD_DIM, HGRN_HEAD_DIM), 0.3),
        'w_in_ab': nrm((ne, D_MODEL, P_AB), D_MODEL ** -0.5),
        'rwkv_mu': jax.random.uniform(next(ks), (ne, RWKV_PROJ), jnp.float32),
        'rwkv_w0': nrm((ne, RWKV_DIM), 0.5),
        'rwkv_w_up': nrm((ne, DECAY_LORA, RWKV_DIM), 0.5 * DECAY_LORA ** -0.5),
        'rwkv_a0': nrm((ne, RWKV_DIM), 0.5),
        'rwkv_a_up': nrm((ne, AAA_LORA, RWKV_DIM), AAA_LORA ** -0.5),
        'rwkv_g_up': nrm((ne, GATE_LORA, RWKV_DIM), GATE_LORA ** -0.5),
        'rwkv_k_k': 0.85 + nrm((ne, RWKV_DIM), 0.05),
        'rwkv_k_a': gain((ne, RWKV_DIM)),
        'rwkv_r_k': nrm((ne, RWKV_HEADS, RWKV_HEAD_DIM), 0.1),
        'rwkv_gn_w': gain((ne, RWKV_DIM)),
        'rwkv_gn_b': nrm((ne, RWKV_DIM), 0.02),
        'gla_gate_up': nrm((ne, GLA_GATE_RANK, GLA_KDIM), GLA_GATE_RANK ** -0.5),
        'gla_gate_b': nrm((ne, GLA_KDIM), 0.1),
        'gla_norm_w': gain((ne, GLA_DV)),
        'w_out_ab': nrm((ne, MIX_AB, D_MODEL), BETA * MIX_AB ** -0.5),
        'ln_ab_g': gain((ne, D_MODEL)),
        'ln_ab_b': nrm((ne, D_MODEL), 0.02),
        'ffn_w1': nrm((ne, D_MODEL, D_FF), D_MODEL ** -0.5),
        'ffn_w3': nrm((ne, D_MODEL, D_FF), D_MODEL ** -0.5),
        'ffn_w2': nrm((ne, D_FF, D_MODEL), BETA * D_FF ** -0.5),
        'ln_ffn_g': gain((ne, D_MODEL)),
        'ln_ffn_b': nrm((ne, D_MODEL), 0.02),
        'w_in_c': nrm((no, D_MODEL, P_C), D_MODEL ** -0.5),
        'hgrn_lb': nrm((DEPTH, HGRN_DIM), 0.5),
        'hgrn_norm_w': gain((no, HGRN_HEAD_DIM)),
        'w_out_c': nrm((no, HGRN_DIM, D_MODEL), BETA * HGRN_DIM ** -0.5),
        'ln_c_g': gain((no, D_MODEL)),
        'ln_c_b': nrm((no, D_MODEL), 0.02),
        'moe_router': nrm((no, D_MODEL, ex), D_MODEL ** -0.5),
        'moe_w1': nrm((no, ex, D_MODEL, EXPERT_D_FF), D_MODEL ** -0.5),
        'moe_w3': nrm((no, ex, D_MODEL, EXPERT_D_FF), D_MODEL ** -0.5),
        'moe_w2': nrm((no, ex, EXPERT_D_FF, D_MODEL), BETA * EXPERT_D_FF ** -0.5),
        'ln_moe_g': gain((no, D_MODEL)),
        'ln_moe_b': nrm((no, D_MODEL), 0.02),
    }


def reference(x_prompt, x_sample, state_rwkv, cache_rwkv_shift, state_gla, state_hgrn,
              w_in_ab, rwkv_mu, rwkv_w0, rwkv_w_up, rwkv_a0, rwkv_a_up, rwkv_g_up, rwkv_k_k, rwkv_k_a,
              rwkv_r_k, rwkv_gn_w, rwkv_gn_b, gla_gate_up, gla_gate_b, gla_norm_w, w_out_ab, ln_ab_g, ln_ab_b,
              ffn_w1, ffn_w3, ffn_w2, ln_ffn_g, ln_ffn_b,
              w_in_c, hgrn_lb, hgrn_norm_w, w_out_c, ln_c_g, ln_c_b,
              moe_router, moe_w1, moe_w3, moe_w2, ln_moe_g, ln_moe_b):
    p = {
        'w_in_ab': w_in_ab, 'rwkv_mu': rwkv_mu, 'rwkv_w0': rwkv_w0, 'rwkv_w_up': rwkv_w_up,
        'rwkv_a0': rwkv_a0, 'rwkv_a_up': rwkv_a_up, 'rwkv_g_up': rwkv_g_up, 'rwkv_k_k': rwkv_k_k,
        'rwkv_k_a': rwkv_k_a, 'rwkv_r_k': rwkv_r_k, 'rwkv_gn_w': rwkv_gn_w, 'rwkv_gn_b': rwkv_gn_b,
        'gla_gate_up': gla_gate_up, 'gla_gate_b': gla_gate_b, 'gla_norm_w': gla_norm_w, 'w_out_ab': w_out_ab,
        'ln_ab_g': ln_ab_g, 'ln_ab_b': ln_ab_b, 'ffn_w1': ffn_w1, 'ffn_w3': ffn_w3, 'ffn_w2': ffn_w2,
        'ln_ffn_g': ln_ffn_g, 'ln_ffn_b': ln_ffn_b, 'w_in_c': w_in_c, 'hgrn_norm_w': hgrn_norm_w,
        'w_out_c': w_out_c, 'ln_c_g': ln_c_g, 'ln_c_b': ln_c_b, 'moe_router': moe_router,
        'moe_w1': moe_w1, 'moe_w3': moe_w3, 'moe_w2': moe_w2, 'ln_moe_g': ln_moe_g, 'ln_moe_b': ln_moe_b,
    }
    s = jax.nn.softmax(hgrn_lb.astype(jnp.float32), axis=0)
    lower_bounds = jnp.cumsum(s, axis=0) - s[0]
    nb = x_prompt.shape[0]
    f32 = jnp.float32
    y_prompt, p_rwkv, p_shift, p_gla, p_hgrn = run_trunk(
        x_prompt,
        jnp.zeros((N_EVEN, nb, RWKV_HEADS, RWKV_HEAD_DIM, RWKV_HEAD_DIM), f32),
        jnp.zeros((N_EVEN, nb, RWKV_PROJ), x_prompt.dtype),
        jnp.zeros((N_EVEN, nb, GLA_HEADS, GLA_DK, GLA_DV), f32),
        jnp.zeros((N_ODD, nb, HGRN_HEADS, HGRN_HEAD_DIM, HGRN_HEAD_DIM), f32),
        lower_bounds, p)
    y_sample, d_rwkv, d_shift, d_gla, d_hgrn = run_trunk(
        x_sample, state_rwkv, cache_rwkv_shift, state_gla, state_hgrn, lower_bounds, p)
    return (y_prompt, y_sample, p_rwkv, p_shift, p_gla, p_hgrn, d_rwkv, d_shift, d_gla, d_hgrn)
```

```python
import functools

import jax
import jax.numpy as jnp
import numpy as np
from jax import lax
from jax.experimental import pallas as pl
from jax.experimental.pallas import tpu as pltpu

F32 = jnp.float32
BF16 = jnp.bfloat16

D_MODEL = 1024
DEPTH = 4
CHUNK = 64
SUB = 16
N_SUB = CHUNK // SUB

RWKV_HEADS = 8
RWKV_HD = 64
RWKV_DIM = RWKV_HEADS * RWKV_HD
DECAY_LORA = 64
AAA_LORA = 64
GATE_LORA = 128
RWKV_PROJ = 3 * RWKV_DIM + DECAY_LORA + AAA_LORA + GATE_LORA
RWKV_GN_EPS = RWKV_HD * 1e-5
DECAY_SCALE = 0.6065306597126334

GLA_HEADS = 4
GLA_DK = 64
GLA_DV = 128
GLA_KDIM = GLA_HEADS * GLA_DK
GLA_VDIM = GLA_HEADS * GLA_DV
GLA_RANK = 16
GLA_TAU = 16.0
GLA_PROJ = 2 * GLA_KDIM + GLA_VDIM + GLA_RANK + GLA_VDIM
GLA_PACK = RWKV_PROJ
GLA_Q0, GLA_K0, GLA_V0, GLA_OG0, GLA_GZ0 = 0, 256, 512, 1024, 1536
GLA_GZW = 128

HGRN_HEADS = 8
HGRN_HD = 128
HGRN_DIM = HGRN_HEADS * HGRN_HD

D_FF = 2816
FF_CHUNK = 256
N_EXPERTS = 8
MOE_BLK = 512

ALPHA = (2 * DEPTH) ** 0.25
NORM_EPS = 1e-5

VMEM_LIMIT = 56 * 1024 * 1024
LANES = 128


def _cparams(*sem):
    return pltpu.CompilerParams(dimension_semantics=tuple(sem), vmem_limit_bytes=VMEM_LIMIT)


def _tile(m, pref):
    t = pref
    while m % t:
        t //= 2
    return t


def _sigmoid(x):
    return 1.0 / (1.0 + jnp.exp(-x))


def _layer_norm(y, g, b):
    mu = jnp.mean(y, axis=-1, keepdims=True)
    d = y - mu
    var = jnp.mean(d * d, axis=-1, keepdims=True)
    return d * lax.rsqrt(var + NORM_EPS) * g + b


def _dot(a, b):
    return jnp.dot(a.astype(BF16), b.astype(BF16), preferred_element_type=F32)


def _dot_nt(a, b):
    return lax.dot_general(a.astype(BF16), b.astype(BF16), (((1,), (1,)), ((), ())),
                           preferred_element_type=F32)


def _dot_tn(a, b):
    return lax.dot_general(a.astype(BF16), b.astype(BF16), (((0,), (0,)), ((), ())),
                           preferred_element_type=F32)


def _mm_kernel(x_ref, w_ref, o_ref):
    o_ref[...] = _dot(x_ref[...], w_ref[...]).astype(o_ref.dtype)


def _matmul(x, w, tm, tn, out_dtype=F32):
    m, k = x.shape
    n = w.shape[1]
    return pl.pallas_call(
        _mm_kernel,
        grid=(m // tm, n // tn),
        in_specs=[pl.BlockSpec((tm, k), lambda i, j: (i, 0)),
                  pl.BlockSpec((k, tn), lambda i, j: (0, j))],
        out_specs=pl.BlockSpec((tm, tn), lambda i, j: (i, j)),
        out_shape=jax.ShapeDtypeStruct((m, n), out_dtype),
        compiler_params=_cparams("parallel", "arbitrary"),
        name="matmul",
    )(x, w)


def _proj_ln_kernel(n_in, *refs):
    a_refs = refs[:n_in]
    w_refs = refs[n_in:2 * n_in]
    res_ref, g_ref, b_ref, o_ref = refs[2 * n_in:]
    acc = _dot(a_refs[0][...], w_refs[0][...])
    for a_ref, w_ref in zip(a_refs[1:], w_refs[1:]):
        acc = acc + _dot(a_ref[...], w_ref[...])
    o_ref[...] = _layer_norm(ALPHA * res_ref[...] + acc, g_ref[...], b_ref[...])


def _proj_ln(acts, ws, res, g, b, tm):
    m, d = res.shape
    n_in = len(acts)
    in_specs = [pl.BlockSpec((tm, a.shape[1]), lambda i: (i, 0)) for a in acts]
    in_specs += [pl.BlockSpec(w.shape, lambda i: (0, 0)) for w in ws]
    in_specs += [pl.BlockSpec((tm, d), lambda i: (i, 0)),
                 pl.BlockSpec((1, d), lambda i: (0, 0)),
                 pl.BlockSpec((1, d), lambda i: (0, 0))]
    return pl.pallas_call(
        functools.partial(_proj_ln_kernel, n_in),
        grid=(m // tm,),
        in_specs=in_specs,
        out_specs=pl.BlockSpec((tm, d), lambda i: (i, 0)),
        out_shape=jax.ShapeDtypeStruct((m, d), F32),
        compiler_params=_cparams("parallel"),
        name="proj_ln",
    )(*acts, *ws, res, g.reshape(1, d), b.reshape(1, d))


def _swiglu_acc(xb, w1_ref, w3_ref, w2_ref, acc_ref):
    acc_ref[...] = jnp.zeros_like(acc_ref)

    def chunk(c, carry):
        c0 = pl.multiple_of(c * FF_CHUNK, FF_CHUNK)
        h1 = jnp.dot(xb, w1_ref[:, pl.ds(c0, FF_CHUNK)], preferred_element_type=F32)
        h3 = jnp.dot(xb, w3_ref[:, pl.ds(c0, FF_CHUNK)], preferred_element_type=F32)
        hh = (h1 * _sigmoid(h1) * h3).astype(BF16)
        acc_ref[...] += jnp.dot(hh, w2_ref[pl.ds(c0, FF_CHUNK), :], preferred_element_type=F32)
        return carry

    lax.fori_loop(0, D_FF // FF_CHUNK, chunk, 0)


def _ffn_ln_kernel(x_ref, w1_ref, w3_ref, w2_ref, g_ref, b_ref, o_ref, acc_ref):
    x = x_ref[...]
    _swiglu_acc(x.astype(BF16), w1_ref, w3_ref, w2_ref, acc_ref)
    o_ref[...] = _layer_norm(ALPHA * x + acc_ref[...], g_ref[...], b_ref[...])


def _ffn_ln(x, w1, w3, w2, g, b, tm):
    m, d = x.shape
    const = lambda i: (0, 0)
    return pl.pallas_call(
        _ffn_ln_kernel,
        grid=(m // tm,),
        in_specs=[pl.BlockSpec((tm, d), lambda i: (i, 0)),
                  pl.BlockSpec(w1.shape, const, pipeline_mode=pl.Buffered(1)),
                  pl.BlockSpec(w3.shape, const, pipeline_mode=pl.Buffered(1)),
                  pl.BlockSpec(w2.shape, const, pipeline_mode=pl.Buffered(1)),
                  pl.BlockSpec((1, d), const),
                  pl.BlockSpec((1, d), const)],
        out_specs=pl.BlockSpec((tm, d), lambda i: (i, 0)),
        out_shape=jax.ShapeDtypeStruct((m, d), F32),
        scratch_shapes=[pltpu.VMEM((tm, d), F32)],
        compiler_params=_cparams("parallel"),
        name="ffn_ln",
    )(x, w1, w3, w2, g.reshape(1, d), b.reshape(1, d))


def _to_rows(x):
    bsz, t_len, d = x.shape
    nb = t_len // CHUNK
    x = x.reshape(bsz // 2, 2, nb, CHUNK, d).transpose(0, 2, 1, 3, 4)
    return x.reshape(bsz * t_len, d)


def _from_rows(y, bsz, t_len):
    d = y.shape[-1]
    nb = t_len // CHUNK
    y = y.reshape(bsz // 2, nb, 2, CHUNK, d).transpose(0, 2, 1, 3, 4)
    return y.reshape(bsz, t_len, d)


def _last_rows(groups):
    rows, row0 = [], 0
    for bsz, t_len in groups:
        nb = t_len // CHUNK
        for b in range(bsz):
            rows.append(row0 + (((b // 2) * nb + nb - 1) * 2 + b % 2) * CHUNK + CHUNK - 1)
        row0 += bsz * t_len
    return np.asarray(rows, np.int32)


def _seq_items(groups, per):
    blk, sidx, flags = [], [], []
    row0, s0 = 0, 0
    for bsz, t_len in groups:
        assert t_len % CHUNK == 0 and bsz % 2 == 0 and row0 % (2 * CHUNK) == 0
        nb = t_len // CHUNK
        for p in range(bsz // 2):
            for q in range(2 // per):
                for t in range(nb):
                    if per == 2:
                        blk.append(row0 // (2 * CHUNK) + p * nb + t)
                        sidx.append(s0 // 2 + p)
                    else:
                        blk.append(row0 // CHUNK + (p * nb + t) * 2 + q)
                        sidx.append(s0 + 2 * p + q)
                    flags.append(int(t == 0) + 2 * int(t == nb - 1))
        row0 += bsz * t_len
        s0 += bsz
    return tuple(jnp.asarray(np.asarray(a, np.int32)) for a in (blk, sidx, flags))


RW_G = 2 * RWKV_HEADS


def _rwkv_kernel(blk_ref, sidx_ref, flag_ref,
                 pr_ref, sh_ref, s0_ref,
                 mu_ref, w0_ref, wup_ref, a0_ref, aup_ref, gup_ref,
                 kk_ref, ka_ref, rk_ref, gnw_ref, gnb_ref,
                 o_ref, st_ref,
                 st_scr, prev_scr, ops_scr, post_scr, oacc_scr):
    i = pl.program_id(0)
    flag = flag_ref[i]
    tt = CHUNK
    rows = 2 * tt

    @pl.when((flag & 1) == 1)
    def _():
        for q in range(2):
            st_scr[q * RWKV_HEADS:(q + 1) * RWKV_HEADS] = s0_ref[q]
            prev_scr[q] = sh_ref[q]

    oacc_scr[...] = jnp.zeros_like(oacc_scr)
    rowi = lax.broadcasted_iota(jnp.int32, (rows, 1), 0)
    lane_head = lax.broadcasted_iota(jnp.int32, (rows, RWKV_DIM), 1) // RWKV_HD

    x = pr_ref[...]
    xprev = jnp.where(rowi == 0, prev_scr[0],
                      jnp.where(rowi == tt, prev_scr[1], pltpu.roll(x, 1, 0)))
    prev_scr[0] = x[tt - 1:tt, :]
    prev_scr[1] = x[rows - 1:rows, :]
    xs = x + (xprev - x) * mu_ref[...]
    r = xs[:, 0:RWKV_DIM]
    k = xs[:, RWKV_DIM:2 * RWKV_DIM]
    v = xs[:, 2 * RWKV_DIM:3 * RWKV_DIM]
    lora = xs[:, 3 * RWKV_DIM:3 * RWKV_DIM + DECAY_LORA + AAA_LORA]
    gd = xs[:, 3 * RWKV_DIM + DECAY_LORA + AAA_LORA:]
    w = jnp.exp(-DECAY_SCALE * _sigmoid(w0_ref[...] + _dot(jnp.tanh(lora), wup_ref[...])))
    a = _sigmoid(a0_ref[...] + _dot(lora, aup_ref[...]))
    g = _dot(_sigmoid(gd), gup_ref[...])
    kkraw = k * kk_ref[...]
    kt = k * (1.0 + (a - 1.0) * ka_ref[...])
    prod = r * kt * rk_ref[...]
    bsum = jnp.zeros((rows, RWKV_DIM), F32)
    for h in range(RWKV_HEADS):
        sl = slice(RWKV_HD * h, RWKV_HD * (h + 1))
        kk_h = kkraw[:, sl]
        nrm = jnp.sqrt(jnp.sum(kk_h * kk_h, axis=1, keepdims=True))
        kk_h = kk_h / jnp.maximum(nrm, 1e-12)
        kka_h = kk_h * a[:, sl]
        for q in range(2):
            gi = q * RWKV_HEADS + h
            rs = slice(q * tt, (q + 1) * tt)
            ops_scr[0, gi] = r[rs, sl]
            ops_scr[1, gi] = w[rs, sl]
            ops_scr[2, gi] = kt[rs, sl]
            ops_scr[3, gi] = v[rs, sl]
            ops_scr[4, gi] = kk_h[rs]
            ops_scr[5, gi] = kka_h[rs]
        bs_h = jnp.sum(prod[:, sl], axis=1, keepdims=True)
        bsum = jnp.where(lane_head == h, bs_h, bsum)
    post_scr[0] = g
    post_scr[1] = bsum * v

    eye = (lax.broadcasted_iota(jnp.int32, (RWKV_HD, RWKV_HD), 0)
           == lax.broadcasted_iota(jnp.int32, (RWKV_HD, RWKV_HD), 1))
    lane_t = lax.broadcasted_iota(jnp.int32, (RWKV_HD, LANES), 1)

    def step(t, carry):
        for gi in range(RW_G):
            s = st_scr[gi]
            r_t = ops_scr[0, gi, pl.ds(t, 1), :]
            w_t = ops_scr[1, gi, pl.ds(t, 1), :]
            k_t = ops_scr[2, gi, pl.ds(t, 1), :]
            v_t = ops_scr[3, gi, pl.ds(t, 1), :]
            kk_t = ops_scr[4, gi, pl.ds(t, 1), :]
            kka_t = ops_scr[5, gi, pl.ds(t, 1), :]
            sa = jnp.sum(s * kk_t, axis=1, keepdims=True)
            v_col = jnp.sum(jnp.where(eye, v_t, 0.0), axis=1, keepdims=True)
            s = s * w_t - sa * kka_t + v_col * k_t
            st_scr[gi] = s
            o_col = jnp.sum(s * r_t, axis=1, keepdims=True)
            oacc_scr[gi] = jnp.where(lane_t == t, o_col, oacc_scr[gi])
        return carry

    lax.fori_loop(0, tt, step, 0)

    def group_norm(o):
        mu = jnp.mean(o, axis=0, keepdims=True)
        d = o - mu
        var = jnp.mean(d * d, axis=0, keepdims=True)
        return d * lax.rsqrt(var + RWKV_GN_EPS)

    halves = []
    for q in range(2):
        pieces = []
        for p in range(RWKV_HEADS // 2):
            pair = jnp.concatenate([group_norm(oacc_scr[q * RWKV_HEADS + 2 * p]),
                                    group_norm(oacc_scr[q * RWKV_HEADS + 2 * p + 1])], axis=0)
            pieces.append(pair.T[:tt, :])
        halves.append(jnp.concatenate(pieces, axis=1))
    on = jnp.concatenate(halves, axis=0)
    out = (on * gnw_ref[...] + gnb_ref[...] + post_scr[1]) * post_scr[0]
    o_ref[...] = out.astype(o_ref.dtype)

    @pl.when((flag & 2) == 2)
    def _():
        for q in range(2):
            st_ref[q] = st_scr[q * RWKV_HEADS:(q + 1) * RWKV_HEADS]


def _rwkv_mix(proj, groups, shift0, s0, params):
    rows = proj.shape[0]
    n_seq = s0.shape[0]
    blk, sidx, flags = _seq_items(groups, 2)
    full = lambda a: pl.BlockSpec(a.shape, lambda i, b, s, f: (0,) * a.ndim)
    st_spec = pl.BlockSpec((2, RWKV_HEADS, RWKV_HD, RWKV_HD), lambda i, b, s, f: (s[i], 0, 0, 0))
    grid_spec = pltpu.PrefetchScalarGridSpec(
        num_scalar_prefetch=3,
        grid=(flags.shape[0],),
        in_specs=[pl.BlockSpec((2 * CHUNK, RWKV_PROJ), lambda i, b, s, f: (b[i], 0)),
                  pl.BlockSpec((2, 1, RWKV_PROJ), lambda i, b, s, f: (s[i], 0, 0)),
                  st_spec] + [full(a) for a in params],
        out_specs=[pl.BlockSpec((2 * CHUNK, RWKV_DIM), lambda i, b, s, f: (b[i], 0)), st_spec],
        scratch_shapes=[pltpu.VMEM((RW_G, RWKV_HD, RWKV_HD), F32),
                        pltpu.VMEM((2, 1, RWKV_PROJ), F32),
                        pltpu.VMEM((6, RW_G, CHUNK, RWKV_HD), F32),
                        pltpu.VMEM((2, 2 * CHUNK, RWKV_DIM), F32),
                        pltpu.VMEM((RW_G, RWKV_HD, LANES), F32)],
    )
    return pl.pallas_call(
        _rwkv_kernel,
        grid_spec=grid_spec,
        out_shape=[jax.ShapeDtypeStruct((rows, RWKV_DIM), BF16),
                   jax.ShapeDtypeStruct((n_seq, RWKV_HEADS, RWKV_HD, RWKV_HD), F32)],
        compiler_params=_cparams("arbitrary"),
        name="rwkv_mix",
    )(blk, sidx, flags, proj, shift0, s0, *params)


def _chunk_consts(dk):
    r = lax.broadcasted_iota(jnp.int32, (CHUNK, CHUNK), 0)
    c = lax.broadcasted_iota(jnp.int32, (CHUNK, CHUNK), 1)
    rb, cb = r // SUB, c // SUB
    return dict(
        tril=(r >= c).astype(F32),
        cross=rb < cb,
        diag=jnp.logical_and(rb == cb, r <= c),
        dsel=c - SUB * rb,
        blkrow=lax.broadcasted_iota(jnp.int32, (CHUNK, 1), 0) // SUB,
        eye=(lax.broadcasted_iota(jnp.int32, (dk, dk), 0)
             == lax.broadcasted_iota(jnp.int32, (dk, dk), 1)),
    )


def _sub_rows(a, off):
    d = a.shape[1]
    return jnp.concatenate(
        [jnp.broadcast_to(a[SUB * i + off:SUB * i + off + 1, :], (SUB, d)) for i in range(N_SUB)], axis=0)


def _chunk_core(q, k, v, g, s, cst):
    dk = q.shape[1]
    b = jnp.dot(cst["tril"], g, precision=lax.Precision.HIGHEST, preferred_element_type=F32)
    o = _dot(q * jnp.exp(b), s)
    e_rows = [jnp.zeros((1, dk), F32)] + [b[SUB * i - 1:SUB * i, :] for i in range(1, N_SUB)]
    e_full = jnp.concatenate([jnp.broadcast_to(e, (SUB, dk)) for e in e_rows], axis=0)
    qg = q * jnp.exp(b - e_full)
    pt_cross = jnp.zeros((CHUNK, CHUNK), F32)
    for i in range(1, N_SUB):
        kg = k * jnp.exp(jnp.minimum(e_rows[i] - b, 0.0))
        pt_cross = pt_cross + _dot_nt(kg, jnp.where(cst["blkrow"] == i, qg, 0.0))
    pt_diag = jnp.zeros((CHUNK, CHUNK), F32)
    for t in range(SUB):
        dec = jnp.exp(jnp.minimum(_sub_rows(b, t) - b, 0.0))
        col = jnp.sum(_sub_rows(q, t) * k * dec, axis=1, keepdims=True)
        pt_diag = jnp.where(cst["dsel"] == t, col, pt_diag)
    pt = jnp.where(cst["cross"], pt_cross, jnp.where(cst["diag"], pt_diag, 0.0))
    o = o + _dot_tn(pt, v)
    b_last = b[CHUNK - 1:CHUNK, :]
    dcol = jnp.sum(jnp.where(cst["eye"], jnp.exp(b_last), 0.0), axis=1, keepdims=True)
    s_new = s * dcol + _dot_tn(k * jnp.exp(b_last - b), v)
    return o, s_new


def _rms_norm(o, w):
    return o * lax.rsqrt(jnp.mean(o * o, axis=-1, keepdims=True) + NORM_EPS) * w


def _log_sigmoid(x):
    return jnp.minimum(x, 0.0) - jnp.log1p(jnp.exp(-jnp.abs(x)))


def _gla_kernel(blk_ref, sidx_ref, flag_ref,
                gl_ref, s0_ref, gup_ref, gb_ref, nw_ref,
                o_ref, st_ref, st_scr):
    i = pl.program_id(0)
    flag = flag_ref[i]

    @pl.when((flag & 1) == 1)
    def _():
        for q in range(2):
            st_scr[q * GLA_HEADS:(q + 1) * GLA_HEADS] = s0_ref[q]

    cst = _chunk_consts(GLA_DK)
    gl = gl_ref[...]
    gz = gl[:, GLA_GZ0:GLA_GZ0 + GLA_GZW]
    log_a = _log_sigmoid(_dot(gz, gup_ref[...]) + gb_ref[...]) / GLA_TAU
    halves = []
    for q in range(2):
        rs = slice(q * CHUNK, (q + 1) * CHUNK)
        outs = []
        for h in range(GLA_HEADS):
            ks = slice(GLA_DK * h, GLA_DK * (h + 1))
            qh = gl[rs, GLA_Q0 + GLA_DK * h:GLA_Q0 + GLA_DK * (h + 1)] * GLA_DK ** -0.5
            kh = gl[rs, GLA_K0 + GLA_DK * h:GLA_K0 + GLA_DK * (h + 1)]
            vh = gl[rs, GLA_V0 + GLA_DV * h:GLA_V0 + GLA_DV * (h + 1)]
            o, s_new = _chunk_core(qh, kh, vh, log_a[rs, ks], st_scr[q * GLA_HEADS + h], cst)
            st_scr[q * GLA_HEADS + h] = s_new
            outs.append(_rms_norm(o, nw_ref[...]))
        halves.append(jnp.concatenate(outs, axis=1))
    on = jnp.concatenate(halves, axis=0)
    og = gl[:, GLA_OG0:GLA_OG0 + GLA_VDIM]
    o_ref[...] = (on * (og * _sigmoid(og))).astype(o_ref.dtype)

    @pl.when((flag & 2) == 2)
    def _():
        for q in range(2):
            st_ref[q] = st_scr[q * GLA_HEADS:(q + 1) * GLA_HEADS]


def _gla_mix(proj, groups, s0, gup, gb, nw):
    rows = proj.shape[0]
    blk, sidx, flags = _seq_items(groups, 2)
    full = lambda a: pl.BlockSpec(a.shape, lambda i, b, s, f: (0,) * a.ndim)
    st_spec = pl.BlockSpec((2, GLA_HEADS, GLA_DK, GLA_DV), lambda i, b, s, f: (s[i], 0, 0, 0))
    grid_spec = pltpu.PrefetchScalarGridSpec(
        num_scalar_prefetch=3,
        grid=(flags.shape[0],),
        in_specs=[pl.BlockSpec((2 * CHUNK, GLA_PACK), lambda i, b, s, f: (b[i], 1)),
                  st_spec, full(gup), full(gb), full(nw)],
        out_specs=[pl.BlockSpec((2 * CHUNK, GLA_VDIM), lambda i, b, s, f: (b[i], 0)), st_spec],
        scratch_shapes=[pltpu.VMEM((2 * GLA_HEADS, GLA_DK, GLA_DV), F32)],
    )
    return pl.pallas_call(
        _gla_kernel,
        grid_spec=grid_spec,
        out_shape=[jax.ShapeDtypeStruct((rows, GLA_VDIM), BF16),
                   jax.ShapeDtypeStruct(s0.shape, F32)],
        compiler_params=_cparams("arbitrary"),
        name="gla_mix",
    )(blk, sidx, flags, proj, s0, gup, gb, nw)


def _hgrn_kernel(blk_ref, sidx_ref, flag_ref,
                 pc_ref, s0_ref, lb_ref, nw_ref,
                 o_ref, st_ref, st_scr):
    i = pl.program_id(0)
    flag = flag_ref[i]

    @pl.when((flag & 1) == 1)
    def _():
        st_scr[...] = s0_ref[0]

    cst = _chunk_consts(HGRN_HD)
    outs = []
    for h in range(HGRN_HEADS):
        hs = slice(HGRN_HD * h, HGRN_HD * (h + 1))
        qx = pc_ref[:, HGRN_HD * h:HGRN_HD * (h + 1)]
        fx = pc_ref[:, HGRN_DIM + HGRN_HD * h:HGRN_DIM + HGRN_HD * (h + 1)]
        ix = pc_ref[:, 2 * HGRN_DIM + HGRN_HD * h:2 * HGRN_DIM + HGRN_HD * (h + 1)]
        lb = lb_ref[:, hs]
        log_f = jnp.log(lb + (1.0 - lb) * _sigmoid(fx))
        kx = (1.0 - lb) * _sigmoid(-fx)
        o, s_new = _chunk_core(qx * _sigmoid(qx), kx, ix, log_f, st_scr[h], cst)
        st_scr[h] = s_new
        outs.append(_rms_norm(o, nw_ref[...]))
    on = jnp.concatenate(outs, axis=1)
    gx = pc_ref[:, 3 * HGRN_DIM:4 * HGRN_DIM]
    o_ref[...] = (on * (gx * _sigmoid(gx))).astype(o_ref.dtype)

    @pl.when((flag & 2) == 2)
    def _():
        st_ref[0] = st_scr[...]


def _hgrn_mix(proj, groups, s0, lb, nw):
    rows = proj.shape[0]
    blk, sidx, flags = _seq_items(groups, 1)
    full = lambda a: pl.BlockSpec(a.shape, lambda i, b, s, f: (0,) * a.ndim)
    st_spec = pl.BlockSpec((1, HGRN_HEADS, HGRN_HD, HGRN_HD), lambda i, b, s, f: (s[i], 0, 0, 0))
    grid_spec = pltpu.PrefetchScalarGridSpec(
        num_scalar_prefetch=3,
        grid=(flags.shape[0],),
        in_specs=[pl.BlockSpec((CHUNK, 4 * HGRN_DIM), lambda i, b, s, f: (b[i], 0)),
                  st_spec, full(lb), full(nw)],
        out_specs=[pl.BlockSpec((CHUNK, HGRN_DIM), lambda i, b, s, f: (b[i], 0)), st_spec],
        scratch_shapes=[pltpu.VMEM((HGRN_HEADS, HGRN_HD, HGRN_HD), F32)],
    )
    return pl.pallas_call(
        _hgrn_kernel,
        grid_spec=grid_spec,
        out_shape=[jax.ShapeDtypeStruct((rows, HGRN_DIM), BF16),
                   jax.ShapeDtypeStruct(s0.shape, F32)],
        compiler_params=_cparams("arbitrary"),
        name="hgrn_mix",
    )(blk, sidx, flags, proj, s0, lb, nw)


ROUTER_TM = 1024
MOE_TQ = 512
META_E1, META_E2, META_P1, META_P2, META_R1, META_R2 = range(6)


def _router_kernel(x_ref, wr_ref, meta_ref, cnt_ref, carry_scr):
    i = pl.program_id(0)

    @pl.when(i == 0)
    def _():
        carry_scr[...] = jnp.zeros_like(carry_scr)

    tm = x_ref.shape[0]
    logits = jnp.dot(x_ref[...], wr_ref[...], precision=lax.Precision.HIGHEST,
                     preferred_element_type=F32)
    lane = lax.broadcasted_iota(jnp.int32, (tm, LANES), 1)
    valid = lane < N_EXPERTS
    lg = jnp.where(valid, logits, -1e30)
    ex = jnp.where(valid, jnp.exp(lg - jnp.max(lg, axis=-1, keepdims=True)), 0.0)
    probs = ex / jnp.sum(ex, axis=-1, keepdims=True)
    p1 = jnp.max(probs, axis=-1, keepdims=True)
    i1 = jnp.min(jnp.where(jnp.logical_and(probs == p1, valid), lane, LANES), axis=-1, keepdims=True)
    rest = jnp.where(jnp.logical_or(lane == i1, jnp.logical_not(valid)), -1.0, probs)
    p2 = jnp.max(rest, axis=-1, keepdims=True)
    i2 = jnp.min(jnp.where(rest == p2, lane, LANES), axis=-1, keepdims=True)
    den = p1 + p2
    oh1 = (lane == i1).astype(F32)
    oh2 = (lane == i2).astype(F32)
    both = oh1 + oh2
    rr = lax.broadcasted_iota(jnp.int32, (tm, tm), 0)
    cc = lax.broadcasted_iota(jnp.int32, (tm, tm), 1)
    before = _dot((rr > cc).astype(F32), both) + carry_scr[...]
    r1 = jnp.sum(oh1 * before, axis=-1, keepdims=True)
    r2 = jnp.sum(oh2 * before, axis=-1, keepdims=True)
    carry_scr[...] += jnp.sum(both, axis=0, keepdims=True)
    meta = jnp.zeros((tm, LANES), F32)
    for idx, col in ((META_E1, i1.astype(F32)), (META_E2, i2.astype(F32)),
                     (META_P1, p1 / den), (META_P2, p2 / den), (META_R1, r1), (META_R2, r2)):
        meta = jnp.where(lane == idx, col, meta)
    meta_ref[...] = meta
    cnt_ref[...] = jnp.broadcast_to(carry_scr[...], cnt_ref.shape)


def _router(x, wr, tm):
    m, d = x.shape
    return pl.pallas_call(
        _router_kernel,
        grid=(m // tm,),
        in_specs=[pl.BlockSpec((tm, d), lambda i: (i, 0)),
                  pl.BlockSpec(wr.shape, lambda i: (0, 0))],
        out_specs=[pl.BlockSpec((tm, LANES), lambda i: (i, 0)),
                   pl.BlockSpec((8, LANES), lambda i: (0, 0))],
        out_shape=[jax.ShapeDtypeStruct((m, LANES), F32),
                   jax.ShapeDtypeStruct((8, LANES), F32)],
        scratch_shapes=[pltpu.VMEM((1, LANES), F32)],
        compiler_params=_cparams("arbitrary"),
        name="moe_router",
    )(x, wr)


def _scatter_kernel(dest_ref, x_ref, xs_in_ref, xs_ref, idx_smem, sem_idx, sem):
    del xs_in_ref
    i = pl.program_id(0)
    tq = idx_smem.shape[0] // 2
    cp = pltpu.make_async_copy(dest_ref.at[i], idx_smem, sem_idx)
    cp.start()
    cp.wait()

    def row_copy(t, slot):
        return pltpu.make_async_copy(x_ref.at[pl.ds(i * tq + t, 1)],
                                     xs_ref.at[pl.ds(idx_smem[slot * tq + t], 1)], sem)

    def issue(t, carry):
        row_copy(t, 0).start()
        row_copy(t, 1).start()
        return carry

    def drain(t, carry):
        row_copy(t, 0).wait()
        row_copy(t, 1).wait()
        return carry

    lax.fori_loop(0, tq, issue, 0)
    lax.fori_loop(0, tq, drain, 0)


def _moe_scatter(x, dest, n_rows, tq):
    m, d = x.shape
    zeros = jnp.zeros((n_rows, d), x.dtype)
    any_spec = pl.BlockSpec(memory_space=pl.ANY)
    return pl.pallas_call(
        _scatter_kernel,
        grid=(m // tq,),
        in_specs=[any_spec, any_spec, any_spec],
        out_specs=any_spec,
        out_shape=jax.ShapeDtypeStruct((n_rows, d), x.dtype),
        scratch_shapes=[pltpu.SMEM((2 * tq,), jnp.int32),
                        pltpu.SemaphoreType.DMA(()),
                        pltpu.SemaphoreType.DMA(())],
        input_output_aliases={2: 0},
        compiler_params=pltpu.CompilerParams(dimension_semantics=("arbitrary",),
                                             has_side_effects=True),
        name="moe_scatter",
    )(dest, x, zeros)


def _expert_kernel(be_ref, bv_ref, x_ref, w1_ref, w3_ref, w2_ref, y_ref, acc_ref):
    b = pl.program_id(0)

    @pl.when(bv_ref[b] == 1)
    def _():
        _swiglu_acc(x_ref[...].astype(BF16), w1_ref, w3_ref, w2_ref, acc_ref)
        y_ref[...] = acc_ref[...]

    @pl.when(bv_ref[b] == 0)
    def _():
        y_ref[...] = jnp.zeros_like(y_ref)


def _moe_experts(xs, block_expert, block_valid, w1, w3, w2):
    rows, d = xs.shape
    wspec = lambda w: pl.BlockSpec((None,) + w.shape[1:], lambda b, be, bv: (be[b], 0, 0))
    grid_spec = pltpu.PrefetchScalarGridSpec(
        num_scalar_prefetch=2,
        grid=(rows // MOE_BLK,),
        in_specs=[pl.BlockSpec((MOE_BLK, d), lambda b, be, bv: (b, 0)),
                  wspec(w1), wspec(w3), wspec(w2)],
        out_specs=pl.BlockSpec((MOE_BLK, d), lambda b, be, bv: (b, 0)),
        scratch_shapes=[pltpu.VMEM((MOE_BLK, d), F32)],
    )
    return pl.pallas_call(
        _expert_kernel,
        grid_spec=grid_spec,
        out_shape=jax.ShapeDtypeStruct((rows, d), F32),
        compiler_params=_cparams("arbitrary"),
        name="moe_experts",
    )(block_expert, block_valid, xs, w1, w3, w2)


def _combine_kernel(dest_ref, y_ref, x_ref, meta_ref, g_ref, b_ref, o_ref,
                    idx_smem, ya_scr, yb_scr, sem_idx, sem):
    i = pl.program_id(0)
    tq = x_ref.shape[0]
    cp = pltpu.make_async_copy(dest_ref.at[i], idx_smem, sem_idx)
    cp.start()
    cp.wait()

    def row_copy(t, slot, buf):
        return pltpu.make_async_copy(y_ref.at[pl.ds(idx_smem[slot * tq + t], 1)],
                                     buf.at[pl.ds(t, 1)], sem)

    def issue(t, carry):
        row_copy(t, 0, ya_scr).start()
        row_copy(t, 1, yb_scr).start()
        return carry

    def drain(t, carry):
        row_copy(t, 0, ya_scr).wait()
        row_copy(t, 1, yb_scr).wait()
        return carry

    lax.fori_loop(0, tq, issue, 0)
    lax.fori_loop(0, tq, drain, 0)
    meta = meta_ref[...]
    moe = ya_scr[...] * meta[:, META_P1:META_P1 + 1] + yb_scr[...] * meta[:, META_P2:META_P2 + 1]
    o_ref[...] = _layer_norm(ALPHA * x_ref[...] + moe, g_ref[...], b_ref[...])


def _moe_combine(dest, y, x, meta, g, b, tq):
    m, d = x.shape
    any_spec = pl.BlockSpec(memory_space=pl.ANY)
    const = lambda i: (0, 0)
    return pl.pallas_call(
        _combine_kernel,
        grid=(m // tq,),
        in_specs=[any_spec, any_spec,
                  pl.BlockSpec((tq, d), lambda i: (i, 0)),
                  pl.BlockSpec((tq, LANES), lambda i: (i, 0)),
                  pl.BlockSpec((1, d), const), pl.BlockSpec((1, d), const)],
        out_specs=pl.BlockSpec((tq, d), lambda i: (i, 0)),
        out_shape=jax.ShapeDtypeStruct((m, d), F32),
        scratch_shapes=[pltpu.SMEM((2 * tq,), jnp.int32),
                        pltpu.VMEM((tq, d), F32), pltpu.VMEM((tq, d), F32),
                        pltpu.SemaphoreType.DMA(()),
                        pltpu.SemaphoreType.DMA(())],
        compiler_params=_cparams("arbitrary"),
        name="moe_combine",
    )(dest, y, x, meta, g.reshape(1, d), b.reshape(1, d))


def _moe_ln(x, router, w1, w3, w2, g, b):
    m, d = x.shape
    tq = _tile(m, MOE_TQ)
    wr = jnp.pad(router, ((0, 0), (0, LANES - N_EXPERTS)))
    meta, cnt = _router(x, wr, _tile(m, ROUTER_TM))
    counts = cnt[0, :N_EXPERTS].astype(jnp.int32)
    padded = (counts + MOE_BLK - 1) // MOE_BLK * MOE_BLK
    pend = jnp.cumsum(padded)
    pstart = pend - padded
    n_blocks = (2 * m) // MOE_BLK + N_EXPERTS
    e1 = meta[:, META_E1].astype(jnp.int32)
    e2 = meta[:, META_E2].astype(jnp.int32)
    onehot = lambda e: (e[:, None] == jnp.arange(N_EXPERTS)[None, :]).astype(jnp.int32)
    dest1 = jnp.sum(onehot(e1) * pstart[None, :], axis=1) + meta[:, META_R1].astype(jnp.int32)
    dest2 = jnp.sum(onehot(e2) * pstart[None, :], axis=1) + meta[:, META_R2].astype(jnp.int32)
    dest = jnp.concatenate([dest1.reshape(m // tq, tq), dest2.reshape(m // tq, tq)], axis=1)
    block_start = jnp.arange(n_blocks, dtype=jnp.int32) * MOE_BLK
    block_expert = jnp.minimum(jnp.sum((block_start[:, None] >= pend[None, :]).astype(jnp.int32), axis=1),
                               N_EXPERTS - 1).astype(jnp.int32)
    block_valid = (block_start < pend[-1]).astype(jnp.int32)
    xs = _moe_scatter(x, dest, n_blocks * MOE_BLK, tq)
    y = _moe_experts(xs, block_expert, block_valid, w1, w3, w2)
    return _moe_combine(dest, y, x, meta, g, b, tq)


def _even_layer(x, j, groups, shift0, s_rwkv, s_gla, p):
    m = x.shape[0]
    w_in = p["w_in_ab"][j]
    pg = w_in[:, RWKV_PROJ:]
    packed = jnp.concatenate(
        [w_in[:, :RWKV_PROJ],
         pg[:, 0:2 * GLA_KDIM + GLA_VDIM],
         pg[:, 2 * GLA_KDIM + GLA_VDIM + GLA_RANK:],
         pg[:, 2 * GLA_KDIM + GLA_VDIM:2 * GLA_KDIM + GLA_VDIM + GLA_RANK],
         jnp.zeros((D_MODEL, GLA_PACK - GLA_PROJ), F32)], axis=1).astype(BF16)
    proj = _matmul(x, packed, _tile(m, 1024), RWKV_PROJ)
    zl = jnp.zeros((DECAY_LORA, RWKV_DIM), F32)
    row = lambda a: a.reshape(1, -1)
    rw_params = [row(p["rwkv_mu"][j]), row(p["rwkv_w0"][j]),
                 jnp.concatenate([p["rwkv_w_up"][j], zl], axis=0).astype(BF16),
                 row(p["rwkv_a0"][j]),
                 jnp.concatenate([zl, p["rwkv_a_up"][j]], axis=0).astype(BF16),
                 p["rwkv_g_up"][j].astype(BF16),
                 row(p["rwkv_k_k"][j]), row(p["rwkv_k_a"][j]), row(p["rwkv_r_k"][j]),
                 row(p["rwkv_gn_w"][j]), row(p["rwkv_gn_b"][j])]
    o_rwkv, s_rwkv = _rwkv_mix(proj, groups, shift0, s_rwkv, rw_params)
    gup = jnp.pad(p["gla_gate_up"][j], ((0, GLA_GZW - GLA_RANK), (0, 0))).astype(BF16)
    o_gla, s_gla = _gla_mix(proj, groups, s_gla, gup, row(p["gla_gate_b"][j]), row(p["gla_norm_w"][j]))
    w_out = p["w_out_ab"][j].astype(BF16)
    x = _proj_ln([o_rwkv, o_gla], [w_out[:RWKV_DIM], w_out[RWKV_DIM:]], x,
                 p["ln_ab_g"][j], p["ln_ab_b"][j], _tile(m, 1024))
    x = _ffn_ln(x, p["ffn_w1"][j].astype(BF16), p["ffn_w3"][j].astype(BF16), p["ffn_w2"][j].astype(BF16),
                p["ln_ffn_g"][j], p["ln_ffn_b"][j], _tile(m, 1024))
    shift_new = proj[_last_rows(groups), :RWKV_PROJ]
    return x, shift_new, s_rwkv, s_gla


def _odd_layer(x, j, groups, lb, s_hgrn, p):
    m = x.shape[0]
    proj = _matmul(x, p["w_in_c"][j].astype(BF16), _tile(m, 1024), 2048)
    o, s_hgrn = _hgrn_mix(proj, groups, s_hgrn, lb.reshape(1, -1), p["hgrn_norm_w"][j].reshape(1, -1))
    x = _proj_ln([o], [p["w_out_c"][j].astype(BF16)], x, p["ln_c_g"][j], p["ln_c_b"][j], _tile(m, 1024))
    x = _moe_ln(x, p["moe_router"][j], p["moe_w1"][j].astype(BF16), p["moe_w3"][j].astype(BF16),
                p["moe_w2"][j].astype(BF16), p["ln_moe_g"][j], p["ln_moe_b"][j])
    return x, s_hgrn


def _trunk(x, groups, s_rwkv, s_shift, s_gla, s_hgrn, lower_bounds, p):
    new_rwkv, new_shift, new_gla, new_hgrn = [], [], [], []
    for layer in range(DEPTH):
        j = layer // 2
        if layer % 2 == 0:
            x, sh, sr, sg = _even_layer(x, j, groups, s_shift[j][:, None, :], s_rwkv[j], s_gla[j], p)
            new_rwkv.append(sr)
            new_shift.append(sh)
            new_gla.append(sg)
        else:
            x, sc = _odd_layer(x, j, groups, lower_bounds[layer], s_hgrn[j], p)
            new_hgrn.append(sc)
    return x, jnp.stack(new_rwkv), jnp.stack(new_shift), jnp.stack(new_gla), jnp.stack(new_hgrn)


def kernel(x_prompt, x_sample, state_rwkv, cache_rwkv_shift, state_gla, state_hgrn,
           w_in_ab, rwkv_mu, rwkv_w0, rwkv_w_up, rwkv_a0, rwkv_a_up, rwkv_g_up, rwkv_k_k, rwkv_k_a,
           rwkv_r_k, rwkv_gn_w, rwkv_gn_b, gla_gate_up, gla_gate_b, gla_norm_w, w_out_ab, ln_ab_g, ln_ab_b,
           ffn_w1, ffn_w3, ffn_w2, ln_ffn_g, ln_ffn_b,
           w_in_c, hgrn_lb, hgrn_norm_w, w_out_c, ln_c_g, ln_c_b,
           moe_router, moe_w1, moe_w3, moe_w2, ln_moe_g, ln_moe_b):
    p = dict(w_in_ab=w_in_ab, rwkv_mu=rwkv_mu, rwkv_w0=rwkv_w0, rwkv_w_up=rwkv_w_up, rwkv_a0=rwkv_a0,
             rwkv_a_up=rwkv_a_up, rwkv_g_up=rwkv_g_up, rwkv_k_k=rwkv_k_k, rwkv_k_a=rwkv_k_a,
             rwkv_r_k=rwkv_r_k, rwkv_gn_w=rwkv_gn_w, rwkv_gn_b=rwkv_gn_b, gla_gate_up=gla_gate_up,
             gla_gate_b=gla_gate_b, gla_norm_w=gla_norm_w, w_out_ab=w_out_ab, ln_ab_g=ln_ab_g,
             ln_ab_b=ln_ab_b, ffn_w1=ffn_w1, ffn_w3=ffn_w3, ffn_w2=ffn_w2, ln_ffn_g=ln_ffn_g,
             ln_ffn_b=ln_ffn_b, w_in_c=w_in_c, hgrn_norm_w=hgrn_norm_w, w_out_c=w_out_c, ln_c_g=ln_c_g,
             ln_c_b=ln_c_b, moe_router=moe_router, moe_w1=moe_w1, moe_w3=moe_w3, moe_w2=moe_w2,
             ln_moe_g=ln_moe_g, ln_moe_b=ln_moe_b)
    sm = jax.nn.softmax(hgrn_lb.astype(F32), axis=0)
    lower_bounds = jnp.cumsum(sm, axis=0) - sm[0]
    pb, pt, _ = x_prompt.shape
    sb, st, _ = x_sample.shape
    groups = [(pb, pt), (sb, st)]
    x = jnp.concatenate([_to_rows(x_prompt), _to_rows(x_sample)], axis=0)

    def with_prompt(state):
        return jnp.concatenate([jnp.zeros((state.shape[0], pb) + state.shape[2:], state.dtype), state], axis=1)

    y, s_rwkv, s_shift, s_gla, s_hgrn = _trunk(
        x, groups, with_prompt(state_rwkv), with_prompt(cache_rwkv_shift), with_prompt(state_gla),
        with_prompt(state_hgrn), lower_bounds, p)
    y_prompt = _from_rows(y[:pb * pt], pb, pt)
    y_sample = _from_rows(y[pb * pt:], sb, st)
    return (y_prompt, y_sample,
            s_rwkv[:, :pb], s_shift[:, :pb], s_gla[:, :pb], s_hgrn[:, :pb],
            s_rwkv[:, pb:], s_shift[:, pb:], s_gla[:, pb:], s_hgrn[:, pb:])
```

```python
import functools

import jax
import jax.numpy as jnp
import numpy as np
from jax import lax
from jax.experimental import pallas as pl
from jax.experimental.pallas import tpu as pltpu

F32 = jnp.float32
BF16 = jnp.bfloat16

D_MODEL = 1024
DEPTH = 4
CHUNK = 64

RWKV_HEADS = 8
RWKV_HD = 64
RWKV_DIM = RWKV_HEADS * RWKV_HD
DECAY_LORA = 64
AAA_LORA = 64
GATE_LORA = 128
RWKV_PROJ = 3 * RWKV_DIM + DECAY_LORA + AAA_LORA + GATE_LORA
RWKV_GN_EPS = RWKV_HD * 1e-5
DECAY_SCALE = 0.6065306597126334

GLA_HEADS = 4
GLA_DK = 64
GLA_DV = 128
GLA_KDIM = GLA_HEADS * GLA_DK
GLA_VDIM = GLA_HEADS * GLA_DV
GLA_RANK = 16
GLA_TAU = 16.0
GLA_GZW = 128

HGRN_HEADS = 8
HGRN_HD = 128
HGRN_DIM = HGRN_HEADS * HGRN_HD

D_FF = 2816
FF_CHUNK = 256
N_EXPERTS = 8
MOE_BLK = 512

ALPHA = (2 * DEPTH) ** 0.25
NORM_EPS = 1e-5

VMEM_LIMIT = 56 * 1024 * 1024
LANES = 128


def _cparams(*sem):
    return pltpu.CompilerParams(dimension_semantics=tuple(sem), vmem_limit_bytes=VMEM_LIMIT)


def _tile(m, pref):
    t = pref
    while m % t:
        t //= 2
    return t


def _sigmoid(x):
    return 1.0 / (1.0 + jnp.exp(-x))


def _layer_norm(y, g, b):
    mu = jnp.mean(y, axis=-1, keepdims=True)
    d = y - mu
    var = jnp.mean(d * d, axis=-1, keepdims=True)
    return d * lax.rsqrt(var + NORM_EPS) * g + b


def _dot(a, b):
    return jnp.dot(a.astype(BF16), b.astype(BF16), preferred_element_type=F32)


def _dot_nt(a, b):
    return lax.dot_general(a.astype(BF16), b.astype(BF16), (((1,), (1,)), ((), ())),
                           preferred_element_type=F32)


def _dot_tn(a, b):
    return lax.dot_general(a.astype(BF16), b.astype(BF16), (((0,), (0,)), ((), ())),
                           preferred_element_type=F32)


def _mm_kernel(x_ref, w_ref, o_ref):
    o_ref[...] = _dot(x_ref[...], w_ref[...]).astype(o_ref.dtype)


def _matmul(x, w, tm, tn, out_dtype=F32):
    m, k = x.shape
    n = w.shape[1]
    return pl.pallas_call(
        _mm_kernel,
        grid=(m // tm, n // tn),
        in_specs=[pl.BlockSpec((tm, k), lambda i, j: (i, 0)),
                  pl.BlockSpec((k, tn), lambda i, j: (0, j))],
        out_specs=pl.BlockSpec((tm, tn), lambda i, j: (i, j)),
        out_shape=jax.ShapeDtypeStruct((m, n), out_dtype),
        compiler_params=_cparams("parallel", "arbitrary"),
        name="matmul",
    )(x, w)


def _proj_ln_kernel(n_in, *refs):
    a_refs = refs[:n_in]
    w_refs = refs[n_in:2 * n_in]
    res_ref, g_ref, b_ref, o_ref = refs[2 * n_in:]
    acc = _dot(a_refs[0][...], w_refs[0][...])
    for a_ref, w_ref in zip(a_refs[1:], w_refs[1:]):
        acc = acc + _dot(a_ref[...], w_ref[...])
    o_ref[...] = _layer_norm(ALPHA * res_ref[...] + acc, g_ref[...], b_ref[...])


def _proj_ln(acts, ws, res, g, b, tm):
    m, d = res.shape
    n_in = len(acts)
    in_specs = [pl.BlockSpec((tm, a.shape[1]), lambda i: (i, 0)) for a in acts]
    in_specs += [pl.BlockSpec(w.shape, lambda i: (0, 0)) for w in ws]
    in_specs += [pl.BlockSpec((tm, d), lambda i: (i, 0)),
                 pl.BlockSpec((1, d), lambda i: (0, 0)),
                 pl.BlockSpec((1, d), lambda i: (0, 0))]
    return pl.pallas_call(
        functools.partial(_proj_ln_kernel, n_in),
        grid=(m // tm,),
        in_specs=in_specs,
        out_specs=pl.BlockSpec((tm, d), lambda i: (i, 0)),
        out_shape=jax.ShapeDtypeStruct((m, d), F32),
        compiler_params=_cparams("parallel"),
        name="proj_ln",
    )(*acts, *ws, res, g.reshape(1, d), b.reshape(1, d))


def _swiglu_acc(xb, w1_ref, w3_ref, w2_ref, acc_ref):
    acc_ref[...] = jnp.zeros_like(acc_ref)

    def chunk(c, carry):
        c0 = pl.multiple_of(c * FF_CHUNK, FF_CHUNK)
        h1 = jnp.dot(xb, w1_ref[:, pl.ds(c0, FF_CHUNK)], preferred_element_type=F32)
        h3 = jnp.dot(xb, w3_ref[:, pl.ds(c0, FF_CHUNK)], preferred_element_type=F32)
        hh = (h1 * _sigmoid(h1) * h3).astype(BF16)
        acc_ref[...] += jnp.dot(hh, w2_ref[pl.ds(c0, FF_CHUNK), :], preferred_element_type=F32)
        return carry

    lax.fori_loop(0, D_FF // FF_CHUNK, chunk, 0)


def _ffn_ln_kernel(x_ref, w1_ref, w3_ref, w2_ref, g_ref, b_ref, o_ref, acc_ref):
    x = x_ref[...]
    _swiglu_acc(x.astype(BF16), w1_ref, w3_ref, w2_ref, acc_ref)
    o_ref[...] = _layer_norm(ALPHA * x + acc_ref[...], g_ref[...], b_ref[...])


def _ffn_ln(x, w1, w3, w2, g, b, tm):
    m, d = x.shape
    const = lambda i: (0, 0)
    return pl.pallas_call(
        _ffn_ln_kernel,
        grid=(m // tm,),
        in_specs=[pl.BlockSpec((tm, d), lambda i: (i, 0)),
                  pl.BlockSpec(w1.shape, const, pipeline_mode=pl.Buffered(1)),
                  pl.BlockSpec(w3.shape, const, pipeline_mode=pl.Buffered(1)),
                  pl.BlockSpec(w2.shape, const, pipeline_mode=pl.Buffered(1)),
                  pl.BlockSpec((1, d), const),
                  pl.BlockSpec((1, d), const)],
        out_specs=pl.BlockSpec((tm, d), lambda i: (i, 0)),
        out_shape=jax.ShapeDtypeStruct((m, d), F32),
        scratch_shapes=[pltpu.VMEM((tm, d), F32)],
        compiler_params=_cparams("parallel"),
        name="ffn_ln",
    )(x, w1, w3, w2, g.reshape(1, d), b.reshape(1, d))


def _to_rows(x):
    bsz, t_len, d = x.shape
    nb = t_len // CHUNK
    x = x.reshape(bsz // 2, 2, nb, CHUNK, d).transpose(0, 2, 1, 3, 4)
    return x.reshape(bsz * t_len, d)


def _from_rows(y, bsz, t_len):
    d = y.shape[-1]
    nb = t_len // CHUNK
    y = y.reshape(bsz // 2, nb, 2, CHUNK, d).transpose(0, 2, 1, 3, 4)
    return y.reshape(bsz, t_len, d)


def _seq_items(groups, per):
    blk, sidx, flags = [], [], []
    row0, s0 = 0, 0
    for bsz, t_len in groups:
        assert t_len % CHUNK == 0 and bsz % 2 == 0 and row0 % (2 * CHUNK) == 0
        nb = t_len // CHUNK
        for p in range(bsz // 2):
            for q in range(2 // per):
                for t in range(nb):
                    if per == 2:
                        blk.append(row0 // (2 * CHUNK) + p * nb + t)
                        sidx.append(s0 // 2 + p)
                    else:
                        blk.append(row0 // CHUNK + (p * nb + t) * 2 + q)
                        sidx.append(s0 + 2 * p + q)
                    flags.append(int(t == 0) + 2 * int(t == nb - 1))
        row0 += bsz * t_len
        s0 += bsz
    return tuple(jnp.asarray(np.asarray(a, np.int32)) for a in (blk, sidx, flags))


RW_G = 2 * RWKV_HEADS
RW_PAIRS = RW_G // 2
RW_UNROLL = 8


def _halves(x, lo):
    sa = jnp.sum(jnp.where(lo, x, 0.0), axis=1, keepdims=True)
    sb = jnp.sum(jnp.where(lo, 0.0, x), axis=1, keepdims=True)
    return jnp.where(lo, sa, sb)


def _rwkv_kernel(blk_ref, sidx_ref, flag_ref,
                 pr_ref, sh_ref, s0_ref,
                 mu_ref, w0_ref, wup_ref, a0_ref, aup_ref, gup_ref,
                 kk_ref, ka_ref, rk_ref, gnw_ref, gnb_ref, onehot_ref,
                 o_ref, st_ref, sho_ref, *scr):
    st = scr[:RW_PAIRS]
    prev_scr, ops_scr, post_scr, vt_scr, vb_scr, oblk_scr = scr[RW_PAIRS:]
    i = pl.program_id(0)
    flag = flag_ref[i]
    tt = CHUNK
    rows = 2 * tt
    hd = RWKV_HD

    @pl.when((flag & 1) == 1)
    def _():
        for q in range(2):
            for p in range(RWKV_HEADS // 2):
                st[q * 4 + p][...] = jnp.concatenate([s0_ref[q, 2 * p], s0_ref[q, 2 * p + 1]], axis=1)
            prev_scr[q] = sh_ref[q]

    rowi = lax.broadcasted_iota(jnp.int32, (rows, 1), 0)
    lo_rows = lax.broadcasted_iota(jnp.int32, (rows, LANES), 1) < hd

    x = pr_ref[...]
    xprev = jnp.where(rowi == 0, prev_scr[0],
                      jnp.where(rowi == tt, prev_scr[1], pltpu.roll(x, 1, 0)))
    prev_scr[0] = x[tt - 1:tt, :]
    prev_scr[1] = x[rows - 1:rows, :]
    xs = x + (xprev - x) * mu_ref[...]
    r = xs[:, 0:RWKV_DIM]
    k = xs[:, RWKV_DIM:2 * RWKV_DIM]
    v = xs[:, 2 * RWKV_DIM:3 * RWKV_DIM]
    lora = xs[:, 3 * RWKV_DIM:3 * RWKV_DIM + DECAY_LORA + AAA_LORA]
    gd = xs[:, 3 * RWKV_DIM + DECAY_LORA + AAA_LORA:]
    w = jnp.exp(-DECAY_SCALE * _sigmoid(w0_ref[...] + _dot(jnp.tanh(lora), wup_ref[...])))
    a = _sigmoid(a0_ref[...] + _dot(lora, aup_ref[...]))
    g = _dot(_sigmoid(gd), gup_ref[...])
    kkraw = k * kk_ref[...]
    kt = k * (1.0 + (a - 1.0) * ka_ref[...])
    prod = r * kt * rk_ref[...]
    bsums = []
    zpad = jnp.zeros((tt, LANES), F32)
    for p in range(RWKV_HEADS // 2):
        cs = slice(LANES * p, LANES * (p + 1))
        kk_p = kkraw[:, cs]
        kk_p = kk_p / jnp.maximum(jnp.sqrt(_halves(kk_p * kk_p, lo_rows)), 1e-12)
        kka_p = kk_p * a[:, cs]
        bsums.append(_halves(prod[:, cs], lo_rows))
        for q in range(2):
            pp = q * 4 + p
            rs = slice(q * tt, (q + 1) * tt)
            ops_scr[0, pp] = r[rs, cs]
            ops_scr[1, pp] = w[rs, cs]
            ops_scr[2, pp] = kt[rs, cs]
            ops_scr[3, pp] = kk_p[rs].astype(BF16).astype(F32)
            ops_scr[4, pp] = kka_p[rs]
            vt = jnp.concatenate([v[rs, cs], zpad], axis=0).T[:, :tt]
            hi = vt.astype(BF16).astype(F32)
            low = vt - hi
            vt_scr[pp * hd:(pp + 1) * hd, :] = jnp.concatenate(
                [hi[:hd], low[:hd], hi[hd:], low[hd:]], axis=1).astype(BF16)
    post_scr[0] = g
    post_scr[1] = jnp.concatenate(bsums, axis=1) * v

    lo = lax.broadcasted_iota(jnp.int32, (hd, LANES), 1) < hd
    lo8 = lax.broadcasted_iota(jnp.int32, (8, LANES), 1) < hd
    row8 = lax.broadcasted_iota(jnp.int32, (8, LANES), 0)
    for c in range(tt // 8):
        cs = slice(8 * LANES * c, 8 * LANES * (c + 1))
        vb_scr[:, cs] = jnp.dot(vt_scr[...], onehot_ref[:, cs], preferred_element_type=F32)

    def step_group(tg, carry):
        t0 = tg * RW_UNROLL
        states = [st[pp][...] for pp in range(RW_PAIRS)]
        rounded = [s.astype(BF16) for s in states]
        for j in range(RW_UNROLL):
            t = t0 + j
            for pp in range(RW_PAIRS):
                s = states[pp]
                r_t = ops_scr[0, pp, pl.ds(t, 1), :]
                w_t = ops_scr[1, pp, pl.ds(t, 1), :]
                k_t = ops_scr[2, pp, pl.ds(t, 1), :]
                kk_t = ops_scr[3, pp, pl.ds(t, 1), :]
                kka_t = ops_scr[4, pp, pl.ds(t, 1), :]
                sa = _halves(rounded[pp].astype(F32) * kk_t, lo)
                v_t = vb_scr[pp * hd:(pp + 1) * hd, pl.ds(pl.multiple_of(t * LANES, LANES), LANES)]
                s = s * w_t - sa * kka_t + v_t * k_t
                states[pp] = s
                rounded[pp] = s.astype(BF16)
                r2 = jnp.where(row8 == 0, jnp.where(lo8, r_t, 0.0),
                               jnp.where(row8 == 1, jnp.where(lo8, 0.0, r_t), 0.0))
                o2 = _dot_nt(r2, rounded[pp])
                oblk_scr[2 * pp, pl.ds(t, 1), :] = o2[0:1]
                oblk_scr[2 * pp + 1, pl.ds(t, 1), :] = o2[1:2]
        for pp in range(RW_PAIRS):
            st[pp][...] = states[pp]
        return carry

    lax.fori_loop(0, tt // RW_UNROLL, step_group, 0)

    def group_norm(o):
        mu = jnp.mean(o, axis=1, keepdims=True)
        d = o - mu
        var = jnp.mean(d * d, axis=1, keepdims=True)
        return d * lax.rsqrt(var + RWKV_GN_EPS)

    on = jnp.concatenate(
        [jnp.concatenate([group_norm(oblk_scr[q * RWKV_HEADS + h]) for h in range(RWKV_HEADS)], axis=1)
         for q in range(2)], axis=0)
    out = (on * gnw_ref[...] + gnb_ref[...] + post_scr[1]) * post_scr[0]
    o_ref[...] = out.astype(o_ref.dtype)

    @pl.when((flag & 2) == 2)
    def _():
        for q in range(2):
            for p in range(RWKV_HEADS // 2):
                s = st[q * 4 + p][...]
                st_ref[q, 2 * p] = s[:, :hd]
                st_ref[q, 2 * p + 1] = s[:, hd:]
            sho_ref[q] = prev_scr[q]


def _value_onehot():
    r = np.arange(4 * CHUNK)[:, None]
    c = np.arange(CHUNK * LANES)[None, :]
    hit = ((r % CHUNK) == (c // LANES)) & ((r // (2 * CHUNK)) == ((c % LANES) // RWKV_HD))
    return jnp.asarray(hit.astype(np.float32), BF16)


def _rwkv_mix(proj, groups, shift0, s0, params):
    rows = proj.shape[0]
    n_seq = s0.shape[0]
    blk, sidx, flags = _seq_items(groups, 2)
    full = lambda a: pl.BlockSpec(a.shape, lambda i, b, s, f: (0,) * a.ndim)
    st_spec = pl.BlockSpec((2, RWKV_HEADS, RWKV_HD, RWKV_HD), lambda i, b, s, f: (s[i], 0, 0, 0))
    sh_spec = pl.BlockSpec((2, 1, RWKV_PROJ), lambda i, b, s, f: (s[i], 0, 0))
    grid_spec = pltpu.PrefetchScalarGridSpec(
        num_scalar_prefetch=3,
        grid=(flags.shape[0],),
        in_specs=[pl.BlockSpec((2 * CHUNK, RWKV_PROJ), lambda i, b, s, f: (b[i], 0)),
                  sh_spec, st_spec] + [full(a) for a in params]
                 + [pl.BlockSpec((4 * CHUNK, CHUNK * LANES), lambda i, b, s, f: (0, 0),
                                 pipeline_mode=pl.Buffered(1))],
        out_specs=[pl.BlockSpec((2 * CHUNK, RWKV_DIM), lambda i, b, s, f: (b[i], 0)), st_spec, sh_spec],
        scratch_shapes=[pltpu.VMEM((RWKV_HD, LANES), F32) for _ in range(RW_PAIRS)]
                       + [pltpu.VMEM((2, 1, RWKV_PROJ), F32),
                          pltpu.VMEM((5, RW_PAIRS, CHUNK, LANES), F32),
                          pltpu.VMEM((2, 2 * CHUNK, RWKV_DIM), F32),
                          pltpu.VMEM((RW_PAIRS * RWKV_HD, 4 * CHUNK), BF16),
                          pltpu.VMEM((RW_PAIRS * RWKV_HD, CHUNK * LANES), F32),
                          pltpu.VMEM((RW_G, CHUNK, RWKV_HD), F32)],
    )
    return pl.pallas_call(
        _rwkv_kernel,
        grid_spec=grid_spec,
        out_shape=[jax.ShapeDtypeStruct((rows, RWKV_DIM), BF16),
                   jax.ShapeDtypeStruct((n_seq, RWKV_HEADS, RWKV_HD, RWKV_HD), F32),
                   jax.ShapeDtypeStruct((n_seq, 1, RWKV_PROJ), F32)],
        compiler_params=_cparams("arbitrary"),
        name="rwkv_mix",
    )(blk, sidx, flags, proj, shift0, s0, *params, _value_onehot())


SUB = 16
LEVELS = (16, 8, 4, 2)
G_PIECES = 3
EX_B, EX_TAIL, EX_Q, EX_K1 = 0, 1, 2, 3
EX_LVL = EX_K1 + CHUNK // SUB - 1
EX_BLOCKS = EX_LVL + len(LEVELS)


def _chunk_tables():
    r = np.arange(CHUNK)[:, None]
    c = np.arange(CHUNK)[None, :]
    mats = [c <= r, c > r, (c >= (r // SUB) * SUB) & (c <= r)]
    for i in range(1, CHUNK // SUB):
        mats.append((c > r) & (c <= SUB * i - 1))
    masks = []
    for size in LEVELS:
        half = size // 2
        mid = (r // size) * size + half - 1
        upper = (r % size) >= half
        mats.append(np.where(upper, (c > mid) & (c <= r), (c > r) & (c <= mid)))
        masks.append(((r // size) == (c // size)) & upper & ((c % size) < half))
    e = np.concatenate(mats, axis=0).astype(np.float32)
    return (jnp.asarray(np.concatenate([e] * G_PIECES, axis=1), BF16),
            jnp.asarray(np.stack(masks).astype(np.float32)))


def _chunk_consts(dk):
    row = lax.broadcasted_iota(jnp.int32, (CHUNK, 1), 0)
    r = lax.broadcasted_iota(jnp.int32, (CHUNK, CHUNK), 0)
    c = lax.broadcasted_iota(jnp.int32, (CHUNK, CHUNK), 1)
    return dict(
        upper=[(row % size) >= size // 2 for size in LEVELS],
        blkrow=row // SUB,
        cross=(r // SUB) > (c // SUB),
        eye_t=r == c,
        eye=(lax.broadcasted_iota(jnp.int32, (dk, dk), 0)
             == lax.broadcasted_iota(jnp.int32, (dk, dk), 1)),
    )


def _split_bf16(x, pieces):
    out = []
    for _ in range(pieces - 1):
        p = x.astype(BF16)
        out.append(p)
        x = x - p.astype(F32)
    out.append(x.astype(BF16))
    return out


def _chunk_core(q, k, v, g, s, ecat_ref, lmask_ref, cst):
    x = jnp.dot(ecat_ref[...], jnp.concatenate(_split_bf16(g, G_PIECES), axis=0),
                preferred_element_type=F32)
    ex = jnp.exp(x)
    blk = lambda i: ex[i * CHUNK:(i + 1) * CHUNK]
    o = _dot(q * blk(EX_B), s)
    qg = q * blk(EX_Q)
    p_cross = jnp.zeros((CHUNK, CHUNK), F32)
    for i in range(1, CHUNK // SUB):
        p_cross = p_cross + _dot_nt(jnp.where(cst["blkrow"] == i, qg, 0.0), k * blk(EX_K1 + i - 1))
    p_diag = jnp.zeros((CHUNK, CHUNK), F32)
    for lvl in range(len(LEVELS)):
        a = jnp.where(cst["upper"][lvl], q, k) * blk(EX_LVL + lvl)
        a_hi, a_lo = _split_bf16(a, 2)
        p_diag = p_diag + lmask_ref[lvl] * lax.dot_general(
            jnp.concatenate([a_hi, a_hi, a_lo], axis=1), jnp.concatenate([a_hi, a_lo, a_hi], axis=1),
            (((1,), (1,)), ((), ())), preferred_element_type=F32)
    p_diag = jnp.where(cst["eye_t"], jnp.sum(q * k, axis=1, keepdims=True), p_diag)
    o = o + _dot(jnp.where(cst["cross"], p_cross, p_diag), v)
    dcol = jnp.sum(jnp.where(cst["eye"], ex[CHUNK - 1:CHUNK], 0.0), axis=1, keepdims=True)
    s_new = s * dcol + _dot_tn(k * blk(EX_TAIL), v)
    return o, s_new


def _rms_norm(o, w):
    return o * lax.rsqrt(jnp.mean(o * o, axis=-1, keepdims=True) + NORM_EPS) * w


def _log_sigmoid(x):
    return jnp.minimum(x, 0.0) - jnp.log1p(jnp.exp(-jnp.abs(x)))


GLA_DKP = 128
GLA_MAIN = 4 * GLA_HEADS * 128


def _gla_kernel(blk_ref, sidx_ref, flag_ref,
                gl_ref, gz_ref, s0_ref, gup_ref, gb_ref, nw_ref, ecat_ref, lmask_ref,
                o_ref, st_ref, *st):
    i = pl.program_id(0)
    flag = flag_ref[i]
    hw = GLA_HEADS * GLA_DKP

    @pl.when((flag & 1) == 1)
    def _():
        for q in range(2):
            for h in range(GLA_HEADS):
                st[q * GLA_HEADS + h][...] = s0_ref[q, h]

    cst = _chunk_consts(GLA_DKP)
    log_a = _log_sigmoid(_dot(gz_ref[...], gup_ref[...]) + gb_ref[...]) / GLA_TAU
    halves = []
    for q in range(2):
        rs = slice(q * CHUNK, (q + 1) * CHUNK)
        outs = []
        for h in range(GLA_HEADS):
            hs = slice(GLA_DKP * h, GLA_DKP * (h + 1))
            qh = gl_ref[rs, GLA_DKP * h:GLA_DKP * (h + 1)] * GLA_DK ** -0.5
            kh = gl_ref[rs, hw + GLA_DKP * h:hw + GLA_DKP * (h + 1)]
            vh = gl_ref[rs, 2 * hw + GLA_DV * h:2 * hw + GLA_DV * (h + 1)]
            sref = st[q * GLA_HEADS + h]
            o, s_new = _chunk_core(qh, kh, vh, log_a[rs, hs], sref[...], ecat_ref, lmask_ref, cst)
            sref[...] = s_new
            outs.append(_rms_norm(o, nw_ref[...]))
        halves.append(jnp.concatenate(outs, axis=1))
    on = jnp.concatenate(halves, axis=0)
    og = gl_ref[:, 3 * hw:3 * hw + GLA_VDIM]
    o_ref[...] = (on * (og * _sigmoid(og))).astype(o_ref.dtype)

    @pl.when((flag & 2) == 2)
    def _():
        for q in range(2):
            for h in range(GLA_HEADS):
                st_ref[q, h] = st[q * GLA_HEADS + h][...]


def _gla_mix(proj, groups, s0, gup, gb, nw, tables):
    rows = proj.shape[0]
    blk, sidx, flags = _seq_items(groups, 2)
    full = lambda a: pl.BlockSpec(a.shape, lambda i, b, s, f: (0,) * a.ndim)
    st_spec = pl.BlockSpec((2, GLA_HEADS, GLA_DKP, GLA_DV), lambda i, b, s, f: (s[i], 0, 0, 0))
    grid_spec = pltpu.PrefetchScalarGridSpec(
        num_scalar_prefetch=3,
        grid=(flags.shape[0],),
        in_specs=[pl.BlockSpec((2 * CHUNK, GLA_MAIN), lambda i, b, s, f: (b[i], 1)),
                  pl.BlockSpec((2 * CHUNK, GLA_GZW), lambda i, b, s, f: (b[i], RWKV_PROJ // GLA_GZW)),
                  st_spec, full(gup), full(gb), full(nw), full(tables[0]), full(tables[1])],
        out_specs=[pl.BlockSpec((2 * CHUNK, GLA_VDIM), lambda i, b, s, f: (b[i], 0)), st_spec],
        scratch_shapes=[pltpu.VMEM((GLA_DKP, GLA_DV), F32) for _ in range(2 * GLA_HEADS)],
    )
    return pl.pallas_call(
        _gla_kernel,
        grid_spec=grid_spec,
        out_shape=[jax.ShapeDtypeStruct((rows, GLA_VDIM), BF16),
                   jax.ShapeDtypeStruct(s0.shape, F32)],
        compiler_params=_cparams("arbitrary"),
        name="gla_mix",
    )(blk, sidx, flags, proj, proj, s0, gup, gb, nw, *tables)


def _hgrn_kernel(blk_ref, sidx_ref, flag_ref,
                 pc_ref, s0_ref, lb_ref, nw_ref, ecat_ref, lmask_ref,
                 o_ref, st_ref, *st):
    i = pl.program_id(0)
    flag = flag_ref[i]

    @pl.when((flag & 1) == 1)
    def _():
        for h in range(HGRN_HEADS):
            st[h][...] = s0_ref[0, h]

    cst = _chunk_consts(HGRN_HD)
    outs = []
    for h in range(HGRN_HEADS):
        hs = slice(HGRN_HD * h, HGRN_HD * (h + 1))
        qx = pc_ref[:, HGRN_HD * h:HGRN_HD * (h + 1)]
        fx = pc_ref[:, HGRN_DIM + HGRN_HD * h:HGRN_DIM + HGRN_HD * (h + 1)]
        ix = pc_ref[:, 2 * HGRN_DIM + HGRN_HD * h:2 * HGRN_DIM + HGRN_HD * (h + 1)]
        lb = lb_ref[:, hs]
        log_f = jnp.log(lb + (1.0 - lb) * _sigmoid(fx))
        kx = (1.0 - lb) * _sigmoid(-fx)
        o, s_new = _chunk_core(qx * _sigmoid(qx), kx, ix, log_f, st[h][...], ecat_ref, lmask_ref, cst)
        st[h][...] = s_new
        outs.append(_rms_norm(o, nw_ref[...]))
    on = jnp.concatenate(outs, axis=1)
    gx = pc_ref[:, 3 * HGRN_DIM:4 * HGRN_DIM]
    o_ref[...] = (on * (gx * _sigmoid(gx))).astype(o_ref.dtype)

    @pl.when((flag & 2) == 2)
    def _():
        for h in range(HGRN_HEADS):
            st_ref[0, h] = st[h][...]


def _hgrn_mix(proj, groups, s0, lb, nw, tables):
    rows = proj.shape[0]
    blk, sidx, flags = _seq_items(groups, 1)
    full = lambda a: pl.BlockSpec(a.shape, lambda i, b, s, f: (0,) * a.ndim)
    st_spec = pl.BlockSpec((1, HGRN_HEADS, HGRN_HD, HGRN_HD), lambda i, b, s, f: (s[i], 0, 0, 0))
    grid_spec = pltpu.PrefetchScalarGridSpec(
        num_scalar_prefetch=3,
        grid=(flags.shape[0],),
        in_specs=[pl.BlockSpec((CHUNK, 4 * HGRN_DIM), lambda i, b, s, f: (b[i], 0)),
                  st_spec, full(lb), full(nw), full(tables[0]), full(tables[1])],
        out_specs=[pl.BlockSpec((CHUNK, HGRN_DIM), lambda i, b, s, f: (b[i], 0)), st_spec],
        scratch_shapes=[pltpu.VMEM((HGRN_HD, HGRN_HD), F32) for _ in range(HGRN_HEADS)],
    )
    return pl.pallas_call(
        _hgrn_kernel,
        grid_spec=grid_spec,
        out_shape=[jax.ShapeDtypeStruct((rows, HGRN_DIM), BF16),
                   jax.ShapeDtypeStruct(s0.shape, F32)],
        compiler_params=_cparams("arbitrary"),
        name="hgrn_mix",
    )(blk, sidx, flags, proj, s0, lb, nw, *tables)


ROUTER_TM = 1024
MOE_TQ = 512
META_E1, META_E2, META_P1, META_P2, META_R1, META_R2 = range(6)


def _router_kernel(x_ref, wr_ref, meta_ref, cnt_ref, carry_scr):
    i = pl.program_id(0)

    @pl.when(i == 0)
    def _():
        carry_scr[...] = jnp.zeros_like(carry_scr)

    tm = x_ref.shape[0]
    logits = _dot(x_ref[...], wr_ref[...])
    lane = lax.broadcasted_iota(jnp.int32, (tm, LANES), 1)
    valid = lane < N_EXPERTS
    lg = jnp.where(valid, logits, -1e30)
    ex = jnp.where(valid, jnp.exp(lg - jnp.max(lg, axis=-1, keepdims=True)), 0.0)
    probs = ex / jnp.sum(ex, axis=-1, keepdims=True)
    p1 = jnp.max(probs, axis=-1, keepdims=True)
    i1 = jnp.min(jnp.where(jnp.logical_and(probs == p1, valid), lane, LANES), axis=-1, keepdims=True)
    rest = jnp.where(jnp.logical_or(lane == i1, jnp.logical_not(valid)), -1.0, probs)
    p2 = jnp.max(rest, axis=-1, keepdims=True)
    i2 = jnp.min(jnp.where(rest == p2, lane, LANES), axis=-1, keepdims=True)
    den = p1 + p2
    oh1 = (lane == i1).astype(F32)
    oh2 = (lane == i2).astype(F32)
    both = oh1 + oh2
    rr = lax.broadcasted_iota(jnp.int32, (tm, tm), 0)
    cc = lax.broadcasted_iota(jnp.int32, (tm, tm), 1)
    before = _dot((rr > cc).astype(F32), both) + carry_scr[...]
    r1 = jnp.sum(oh1 * before, axis=-1, keepdims=True)
    r2 = jnp.sum(oh2 * before, axis=-1, keepdims=True)
    carry_scr[...] += jnp.sum(both, axis=0, keepdims=True)
    meta = jnp.zeros((tm, LANES), F32)
    for idx, col in ((META_E1, i1.astype(F32)), (META_E2, i2.astype(F32)),
                     (META_P1, p1 / den), (META_P2, p2 / den), (META_R1, r1), (META_R2, r2)):
        meta = jnp.where(lane == idx, col, meta)
    meta_ref[...] = meta
    cnt_ref[...] = jnp.broadcast_to(carry_scr[...], cnt_ref.shape)


def _router(x, wr, tm):
    m, d = x.shape
    return pl.pallas_call(
        _router_kernel,
        grid=(m // tm,),
        in_specs=[pl.BlockSpec((tm, d), lambda i: (i, 0)),
                  pl.BlockSpec(wr.shape, lambda i: (0, 0))],
        out_specs=[pl.BlockSpec((tm, LANES), lambda i: (i, 0)),
                   pl.BlockSpec((8, LANES), lambda i: (0, 0))],
        out_shape=[jax.ShapeDtypeStruct((m, LANES), F32),
                   jax.ShapeDtypeStruct((8, LANES), F32)],
        scratch_shapes=[pltpu.VMEM((1, LANES), F32)],
        compiler_params=_cparams("arbitrary"),
        name="moe_router",
    )(x, wr)


def _scatter_kernel(dest_ref, x_ref, xs_in_ref, xs_ref, idx_smem, sem_idx, sem):
    del xs_in_ref
    i = pl.program_id(0)
    tq = x_ref.shape[0]
    cp = pltpu.make_async_copy(dest_ref.at[i], idx_smem, sem_idx)
    cp.start()
    cp.wait()

    def row_copy(t, slot):
        return pltpu.make_async_copy(x_ref.at[pl.ds(t, 1)],
                                     xs_ref.at[pl.ds(idx_smem[slot * tq + t], 1)], sem)

    def issue(t, carry):
        row_copy(t, 0).start()
        row_copy(t, 1).start()
        return carry

    def drain(t, carry):
        row_copy(t, 0).wait()
        row_copy(t, 1).wait()
        return carry

    lax.fori_loop(0, tq, issue, 0)
    lax.fori_loop(0, tq, drain, 0)


def _moe_scatter(x, dest, n_rows, tq):
    m, d = x.shape
    zeros = jnp.zeros((n_rows, d), x.dtype)
    any_spec = pl.BlockSpec(memory_space=pl.ANY)
    return pl.pallas_call(
        _scatter_kernel,
        grid=(m // tq,),
        in_specs=[any_spec, pl.BlockSpec((tq, d), lambda i: (i, 0)), any_spec],
        out_specs=any_spec,
        out_shape=jax.ShapeDtypeStruct((n_rows, d), x.dtype),
        scratch_shapes=[pltpu.SMEM((2 * tq,), jnp.int32),
                        pltpu.SemaphoreType.DMA(()),
                        pltpu.SemaphoreType.DMA(())],
        input_output_aliases={2: 0},
        compiler_params=pltpu.CompilerParams(dimension_semantics=("arbitrary",),
                                             has_side_effects=True),
        name="moe_scatter",
    )(dest, x, zeros)


def _expert_kernel(be_ref, bv_ref, x_ref, w1_ref, w3_ref, w2_ref, y_ref, acc_ref):
    b = pl.program_id(0)

    @pl.when(bv_ref[b] == 1)
    def _():
        _swiglu_acc(x_ref[...].astype(BF16), w1_ref, w3_ref, w2_ref, acc_ref)
        y_ref[...] = acc_ref[...]

    @pl.when(bv_ref[b] == 0)
    def _():
        y_ref[...] = jnp.zeros_like(y_ref)


def _moe_experts(xs, block_expert, block_valid, w1, w3, w2):
    rows, d = xs.shape
    wspec = lambda w: pl.BlockSpec((None,) + w.shape[1:], lambda b, be, bv: (be[b], 0, 0))
    grid_spec = pltpu.PrefetchScalarGridSpec(
        num_scalar_prefetch=2,
        grid=(rows // MOE_BLK,),
        in_specs=[pl.BlockSpec((MOE_BLK, d), lambda b, be, bv: (b, 0)),
                  wspec(w1), wspec(w3), wspec(w2)],
        out_specs=pl.BlockSpec((MOE_BLK, d), lambda b, be, bv: (b, 0)),
        scratch_shapes=[pltpu.VMEM((MOE_BLK, d), F32)],
    )
    return pl.pallas_call(
        _expert_kernel,
        grid_spec=grid_spec,
        out_shape=jax.ShapeDtypeStruct((rows, d), F32),
        compiler_params=_cparams("arbitrary"),
        name="moe_experts",
    )(block_expert, block_valid, xs, w1, w3, w2)


def _combine_kernel(dest_ref, y_ref, x_ref, meta_ref, g_ref, b_ref, o_ref,
                    idx_smem, ya_scr, yb_scr, sem_idx, sem):
    i = pl.program_id(0)
    tq = x_ref.shape[0]
    cp = pltpu.make_async_copy(dest_ref.at[i], idx_smem, sem_idx)
    cp.start()
    cp.wait()

    def row_copy(t, slot, buf):
        return pltpu.make_async_copy(y_ref.at[pl.ds(idx_smem[slot * tq + t], 1)],
                                     buf.at[pl.ds(t, 1)], sem)

    def issue(t, carry):
        row_copy(t, 0, ya_scr).start()
        row_copy(t, 1, yb_scr).start()
        return carry

    def drain(t, carry):
        row_copy(t, 0, ya_scr).wait()
        row_copy(t, 1, yb_scr).wait()
        return carry

    lax.fori_loop(0, tq, issue, 0)
    lax.fori_loop(0, tq, drain, 0)
    meta = meta_ref[...]
    moe = ya_scr[...] * meta[:, META_P1:META_P1 + 1] + yb_scr[...] * meta[:, META_P2:META_P2 + 1]
    o_ref[...] = _layer_norm(ALPHA * x_ref[...] + moe, g_ref[...], b_ref[...])


def _moe_combine(dest, y, x, meta, g, b, tq):
    m, d = x.shape
    any_spec = pl.BlockSpec(memory_space=pl.ANY)
    const = lambda i: (0, 0)
    return pl.pallas_call(
        _combine_kernel,
        grid=(m // tq,),
        in_specs=[any_spec, any_spec,
                  pl.BlockSpec((tq, d), lambda i: (i, 0)),
                  pl.BlockSpec((tq, LANES), lambda i: (i, 0)),
                  pl.BlockSpec((1, d), const), pl.BlockSpec((1, d), const)],
        out_specs=pl.BlockSpec((tq, d), lambda i: (i, 0)),
        out_shape=jax.ShapeDtypeStruct((m, d), F32),
        scratch_shapes=[pltpu.SMEM((2 * tq,), jnp.int32),
                        pltpu.VMEM((tq, d), F32), pltpu.VMEM((tq, d), F32),
                        pltpu.SemaphoreType.DMA(()),
                        pltpu.SemaphoreType.DMA(())],
        compiler_params=_cparams("arbitrary"),
        name="moe_combine",
    )(dest, y, x, meta, g.reshape(1, d), b.reshape(1, d))


def _moe_ln(x, router, w1, w3, w2, g, b):
    m, d = x.shape
    tq = _tile(m, MOE_TQ)
    wr = jnp.pad(router, ((0, 0), (0, LANES - N_EXPERTS)))
    meta, cnt = _router(x, wr, _tile(m, ROUTER_TM))
    counts = cnt[0, :N_EXPERTS].astype(jnp.int32)
    padded = (counts + MOE_BLK - 1) // MOE_BLK * MOE_BLK
    pend = jnp.cumsum(padded)
    pstart = pend - padded
    n_blocks = (2 * m) // MOE_BLK + N_EXPERTS
    e1 = meta[:, META_E1].astype(jnp.int32)
    e2 = meta[:, META_E2].astype(jnp.int32)
    onehot = lambda e: (e[:, None] == jnp.arange(N_EXPERTS)[None, :]).astype(jnp.int32)
    dest1 = jnp.sum(onehot(e1) * pstart[None, :], axis=1) + meta[:, META_R1].astype(jnp.int32)
    dest2 = jnp.sum(onehot(e2) * pstart[None, :], axis=1) + meta[:, META_R2].astype(jnp.int32)
    dest = jnp.concatenate([dest1.reshape(m // tq, tq), dest2.reshape(m // tq, tq)], axis=1)
    block_start = jnp.arange(n_blocks, dtype=jnp.int32) * MOE_BLK
    block_expert = jnp.minimum(jnp.sum((block_start[:, None] >= pend[None, :]).astype(jnp.int32), axis=1),
                               N_EXPERTS - 1).astype(jnp.int32)
    block_valid = (block_start < pend[-1]).astype(jnp.int32)
    xs = _moe_scatter(x, dest, n_blocks * MOE_BLK, tq)
    y = _moe_experts(xs, block_expert, block_valid, w1, w3, w2)
    return _moe_combine(dest, y, x, meta, g, b, tq)


def _even_layer(x, j, groups, shift0, s_rwkv, s_gla, tables, p):
    m = x.shape[0]
    w_in = p["w_in_ab"][j]
    pg = w_in[:, RWKV_PROJ:]
    zcol = lambda n: jnp.zeros((D_MODEL, n), F32)

    def pad_heads(wq):
        wq = wq.reshape(D_MODEL, GLA_HEADS, GLA_DK)
        return jnp.pad(wq, ((0, 0), (0, 0), (0, GLA_DKP - GLA_DK))).reshape(D_MODEL, GLA_HEADS * GLA_DKP)

    gz0 = 2 * GLA_KDIM + GLA_VDIM
    packed = jnp.concatenate(
        [w_in[:, :RWKV_PROJ],
         pg[:, gz0:gz0 + GLA_RANK], zcol(2 * GLA_GZW - GLA_RANK),
         pad_heads(pg[:, 0:GLA_KDIM]), pad_heads(pg[:, GLA_KDIM:2 * GLA_KDIM]),
         pg[:, 2 * GLA_KDIM:gz0],
         pg[:, gz0 + GLA_RANK:]], axis=1).astype(BF16)
    proj = _matmul(x, packed, _tile(m, 1024), 2048)
    zl = jnp.zeros((DECAY_LORA, RWKV_DIM), F32)
    row = lambda a: a.reshape(1, -1)
    rw_params = [row(p["rwkv_mu"][j]), row(p["rwkv_w0"][j]),
                 jnp.concatenate([p["rwkv_w_up"][j], zl], axis=0).astype(BF16),
                 row(p["rwkv_a0"][j]),
                 jnp.concatenate([zl, p["rwkv_a_up"][j]], axis=0).astype(BF16),
                 p["rwkv_g_up"][j].astype(BF16),
                 row(p["rwkv_k_k"][j]), row(p["rwkv_k_a"][j]), row(p["rwkv_r_k"][j]),
                 row(p["rwkv_gn_w"][j]), row(p["rwkv_gn_b"][j])]
    o_rwkv, s_rwkv, shift_new = _rwkv_mix(proj, groups, shift0, s_rwkv, rw_params)
    pad_k = lambda a: jnp.pad(a.reshape(-1, GLA_HEADS, GLA_DK),
                              ((0, 0), (0, 0), (0, GLA_DKP - GLA_DK))).reshape(-1, GLA_HEADS * GLA_DKP)
    gup = jnp.pad(pad_k(p["gla_gate_up"][j]), ((0, GLA_GZW - GLA_RANK), (0, 0))).astype(BF16)
    s_gla_p = jnp.pad(s_gla, ((0, 0), (0, 0), (0, GLA_DKP - GLA_DK), (0, 0)))
    o_gla, s_gla_p = _gla_mix(proj, groups, s_gla_p, gup, pad_k(row(p["gla_gate_b"][j])),
                              row(p["gla_norm_w"][j]), tables)
    w_out = p["w_out_ab"][j].astype(BF16)
    x = _proj_ln([o_rwkv, o_gla], [w_out[:RWKV_DIM], w_out[RWKV_DIM:]], x,
                 p["ln_ab_g"][j], p["ln_ab_b"][j], _tile(m, 1024))
    x = _ffn_ln(x, p["ffn_w1"][j].astype(BF16), p["ffn_w3"][j].astype(BF16), p["ffn_w2"][j].astype(BF16),
                p["ln_ffn_g"][j], p["ln_ffn_b"][j], _tile(m, 1024))
    return x, shift_new[:, 0, :], s_rwkv, s_gla_p[:, :, :GLA_DK, :]


def _odd_layer(x, j, groups, lb, s_hgrn, tables, p):
    m = x.shape[0]
    proj = _matmul(x, p["w_in_c"][j].astype(BF16), _tile(m, 1024), 2048)
    o, s_hgrn = _hgrn_mix(proj, groups, s_hgrn, lb.reshape(1, -1), p["hgrn_norm_w"][j].reshape(1, -1), tables)
    x = _proj_ln([o], [p["w_out_c"][j].astype(BF16)], x, p["ln_c_g"][j], p["ln_c_b"][j], _tile(m, 1024))
    x = _moe_ln(x, p["moe_router"][j], p["moe_w1"][j].astype(BF16), p["moe_w3"][j].astype(BF16),
                p["moe_w2"][j].astype(BF16), p["ln_moe_g"][j], p["ln_moe_b"][j])
    return x, s_hgrn


def _trunk(x, groups, s_rwkv, s_shift, s_gla, s_hgrn, lower_bounds, p):
    new_rwkv, new_shift, new_gla, new_hgrn = [], [], [], []
    tables = _chunk_tables()
    for layer in range(DEPTH):
        j = layer // 2
        if layer % 2 == 0:
            x, sh, sr, sg = _even_layer(x, j, groups, s_shift[j][:, None, :], s_rwkv[j], s_gla[j], tables, p)
            new_rwkv.append(sr)
            new_shift.append(sh)
            new_gla.append(sg)
        else:
            x, sc = _odd_layer(x, j, groups, lower_bounds[layer], s_hgrn[j], tables, p)
            new_hgrn.append(sc)
    return x, jnp.stack(new_rwkv), jnp.stack(new_shift), jnp.stack(new_gla), jnp.stack(new_hgrn)


def kernel(x_prompt, x_sample, state_rwkv, cache_rwkv_shift, state_gla, state_hgrn,
           w_in_ab, rwkv_mu, rwkv_w0, rwkv_w_up, rwkv_a0, rwkv_a_up, rwkv_g_up, rwkv_k_k, rwkv_k_a,
           rwkv_r_k, rwkv_gn_w, rwkv_gn_b, gla_gate_up, gla_gate_b, gla_norm_w, w_out_ab, ln_ab_g, ln_ab_b,
           ffn_w1, ffn_w3, ffn_w2, ln_ffn_g, ln_ffn_b,
           w_in_c, hgrn_lb, hgrn_norm_w, w_out_c, ln_c_g, ln_c_b,
           moe_router, moe_w1, moe_w3, moe_w2, ln_moe_g, ln_moe_b):
    p = dict(w_in_ab=w_in_ab, rwkv_mu=rwkv_mu, rwkv_w0=rwkv_w0, rwkv_w_up=rwkv_w_up, rwkv_a0=rwkv_a0,
             rwkv_a_up=rwkv_a_up, rwkv_g_up=rwkv_g_up, rwkv_k_k=rwkv_k_k, rwkv_k_a=rwkv_k_a,
             rwkv_r_k=rwkv_r_k, rwkv_gn_w=rwkv_gn_w, rwkv_gn_b=rwkv_gn_b, gla_gate_up=gla_gate_up,
             gla_gate_b=gla_gate_b, gla_norm_w=gla_norm_w, w_out_ab=w_out_ab, ln_ab_g=ln_ab_g,
             ln_ab_b=ln_ab_b, ffn_w1=ffn_w1, ffn_w3=ffn_w3, ffn_w2=ffn_w2, ln_ffn_g=ln_ffn_g,
             ln_ffn_b=ln_ffn_b, w_in_c=w_in_c, hgrn_norm_w=hgrn_norm_w, w_out_c=w_out_c, ln_c_g=ln_c_g,
             ln_c_b=ln_c_b, moe_router=moe_router, moe_w1=moe_w1, moe_w3=moe_w3, moe_w2=moe_w2,
             ln_moe_g=ln_moe_g, ln_moe_b=ln_moe_b)
    sm = jax.nn.softmax(hgrn_lb.astype(F32), axis=0)
    lower_bounds = jnp.cumsum(sm, axis=0) - sm[0]
    pb, pt, _ = x_prompt.shape
    sb, st, _ = x_sample.shape
    groups = [(pb, pt), (sb, st)]
    x = jnp.concatenate([_to_rows(x_prompt), _to_rows(x_sample)], axis=0)

    def with_prompt(state):
        return jnp.concatenate([jnp.zeros((state.shape[0], pb) + state.shape[2:], state.dtype), state], axis=1)

    y, s_rwkv, s_shift, s_gla, s_hgrn = _trunk(
        x, groups, with_prompt(state_rwkv), with_prompt(cache_rwkv_shift), with_prompt(state_gla),
        with_prompt(state_hgrn), lower_bounds, p)
    y_prompt = _from_rows(y[:pb * pt], pb, pt)
    y_sample = _from_rows(y[pb * pt:], sb, st)
    return (y_prompt, y_sample,
            s_rwkv[:, :pb], s_shift[:, :pb], s_gla[:, :pb], s_hgrn[:, :pb],
            s_rwkv[:, pb:], s_shift[:, pb:], s_gla[:, pb:], s_hgrn[:, pb:])
```

```python
import functools

import jax
import jax.numpy as jnp
import numpy as np
from jax import lax
from jax.experimental import pallas as pl
from jax.experimental.pallas import tpu as pltpu

F32 = jnp.float32
BF16 = jnp.bfloat16

D_MODEL = 1024
DEPTH = 4
CHUNK = 64

RWKV_HEADS = 8
RWKV_HD = 64
RWKV_DIM = RWKV_HEADS * RWKV_HD
DECAY_LORA = 64
AAA_LORA = 64
GATE_LORA = 128
RWKV_PROJ = 3 * RWKV_DIM + DECAY_LORA + AAA_LORA + GATE_LORA
RWKV_GN_EPS = RWKV_HD * 1e-5
DECAY_SCALE = 0.6065306597126334

GLA_HEADS = 4
GLA_DK = 64
GLA_DV = 128
GLA_KDIM = GLA_HEADS * GLA_DK
GLA_VDIM = GLA_HEADS * GLA_DV
GLA_RANK = 16
GLA_TAU = 16.0
GLA_GZW = 128

HGRN_HEADS = 8
HGRN_HD = 128
HGRN_DIM = HGRN_HEADS * HGRN_HD

D_FF = 2816
FF_CHUNK = 256
N_EXPERTS = 8
MOE_BLK = 512

ALPHA = (2 * DEPTH) ** 0.25
NORM_EPS = 1e-5

VMEM_LIMIT = 56 * 1024 * 1024
LANES = 128


def _cparams(*sem):
    return pltpu.CompilerParams(dimension_semantics=tuple(sem), vmem_limit_bytes=VMEM_LIMIT)


def _tile(m, pref):
    t = pref
    while m % t:
        t //= 2
    return t


def _sigmoid(x):
    return 1.0 / (1.0 + jnp.exp(-x))


def _layer_norm(y, g, b):
    mu = jnp.mean(y, axis=-1, keepdims=True)
    d = y - mu
    var = jnp.mean(d * d, axis=-1, keepdims=True)
    return d * lax.rsqrt(var + NORM_EPS) * g + b


def _dot(a, b):
    return jnp.dot(a.astype(BF16), b.astype(BF16), preferred_element_type=F32)


def _dot_nt(a, b):
    return lax.dot_general(a.astype(BF16), b.astype(BF16), (((1,), (1,)), ((), ())),
                           preferred_element_type=F32)


def _dot_tn(a, b):
    return lax.dot_general(a.astype(BF16), b.astype(BF16), (((0,), (0,)), ((), ())),
                           preferred_element_type=F32)


def _mm_kernel(x_ref, w_ref, o_ref):
    o_ref[...] = _dot(x_ref[...], w_ref[...]).astype(o_ref.dtype)


def _matmul(x, w, tm, tn, out_dtype=F32):
    m, k = x.shape
    n = w.shape[1]
    return pl.pallas_call(
        _mm_kernel,
        grid=(m // tm, n // tn),
        in_specs=[pl.BlockSpec((tm, k), lambda i, j: (i, 0)),
                  pl.BlockSpec((k, tn), lambda i, j: (0, j))],
        out_specs=pl.BlockSpec((tm, tn), lambda i, j: (i, j)),
        out_shape=jax.ShapeDtypeStruct((m, n), out_dtype),
        compiler_params=_cparams("parallel", "arbitrary"),
        name="matmul",
    )(x, w)


def _proj_ln_kernel(n_in, *refs):
    a_refs = refs[:n_in]
    w_refs = refs[n_in:2 * n_in]
    res_ref, g_ref, b_ref, o_ref = refs[2 * n_in:]
    acc = _dot(a_refs[0][...], w_refs[0][...])
    for a_ref, w_ref in zip(a_refs[1:], w_refs[1:]):
        acc = acc + _dot(a_ref[...], w_ref[...])
    o_ref[...] = _layer_norm(ALPHA * res_ref[...] + acc, g_ref[...], b_ref[...])


def _proj_ln(acts, ws, res, g, b, tm):
    m, d = res.shape
    n_in = len(acts)
    in_specs = [pl.BlockSpec((tm, a.shape[1]), lambda i: (i, 0)) for a in acts]
    in_specs += [pl.BlockSpec(w.shape, lambda i: (0, 0)) for w in ws]
    in_specs += [pl.BlockSpec((tm, d), lambda i: (i, 0)),
                 pl.BlockSpec((1, d), lambda i: (0, 0)),
                 pl.BlockSpec((1, d), lambda i: (0, 0))]
    return pl.pallas_call(
        functools.partial(_proj_ln_kernel, n_in),
        grid=(m // tm,),
        in_specs=in_specs,
        out_specs=pl.BlockSpec((tm, d), lambda i: (i, 0)),
        out_shape=jax.ShapeDtypeStruct((m, d), F32),
        compiler_params=_cparams("parallel"),
        name="proj_ln",
    )(*acts, *ws, res, g.reshape(1, d), b.reshape(1, d))


def _swiglu_acc(xb, w1_ref, w3_ref, w2_ref, acc_ref):
    acc_ref[...] = jnp.zeros_like(acc_ref)

    def chunk(c, carry):
        c0 = pl.multiple_of(c * FF_CHUNK, FF_CHUNK)
        h1 = jnp.dot(xb, w1_ref[:, pl.ds(c0, FF_CHUNK)].astype(BF16), preferred_element_type=F32)
        h3 = jnp.dot(xb, w3_ref[:, pl.ds(c0, FF_CHUNK)].astype(BF16), preferred_element_type=F32)
        hh = (h1 * _sigmoid(h1) * h3).astype(BF16)
        acc_ref[...] += jnp.dot(hh, w2_ref[pl.ds(c0, FF_CHUNK), :].astype(BF16), preferred_element_type=F32)
        return carry

    lax.fori_loop(0, D_FF // FF_CHUNK, chunk, 0)


def _ffn_ln_kernel(x_ref, w1_ref, w3_ref, w2_ref, g_ref, b_ref, o_ref, acc_ref):
    x = x_ref[...]
    _swiglu_acc(x.astype(BF16), w1_ref, w3_ref, w2_ref, acc_ref)
    o_ref[...] = _layer_norm(ALPHA * x + acc_ref[...], g_ref[...], b_ref[...])


def _ffn_ln(x, w1, w3, w2, g, b, tm):
    m, d = x.shape
    const = lambda i: (0, 0)
    return pl.pallas_call(
        _ffn_ln_kernel,
        grid=(m // tm,),
        in_specs=[pl.BlockSpec((tm, d), lambda i: (i, 0)),
                  pl.BlockSpec(w1.shape, const, pipeline_mode=pl.Buffered(1)),
                  pl.BlockSpec(w3.shape, const, pipeline_mode=pl.Buffered(1)),
                  pl.BlockSpec(w2.shape, const, pipeline_mode=pl.Buffered(1)),
                  pl.BlockSpec((1, d), const),
                  pl.BlockSpec((1, d), const)],
        out_specs=pl.BlockSpec((tm, d), lambda i: (i, 0)),
        out_shape=jax.ShapeDtypeStruct((m, d), F32),
        scratch_shapes=[pltpu.VMEM((tm, d), F32)],
        compiler_params=_cparams("parallel"),
        name="ffn_ln",
    )(x, w1, w3, w2, g.reshape(1, d), b.reshape(1, d))


def _to_rows(x):
    bsz, t_len, d = x.shape
    nb = t_len // CHUNK
    x = x.reshape(bsz // 2, 2, nb, CHUNK, d).transpose(0, 2, 1, 3, 4)
    return x.reshape(bsz * t_len, d)


def _from_rows(y, bsz, t_len):
    d = y.shape[-1]
    nb = t_len // CHUNK
    y = y.reshape(bsz // 2, nb, 2, CHUNK, d).transpose(0, 2, 1, 3, 4)
    return y.reshape(bsz, t_len, d)


def _seq_items(groups, per):
    blk, sidx, flags = [], [], []
    row0, s0 = 0, 0
    for bsz, t_len in groups:
        assert t_len % CHUNK == 0 and bsz % 2 == 0 and row0 % (2 * CHUNK) == 0
        nb = t_len // CHUNK
        for p in range(bsz // 2):
            for q in range(2 // per):
                for t in range(nb):
                    if per == 2:
                        blk.append(row0 // (2 * CHUNK) + p * nb + t)
                        sidx.append(s0 // 2 + p)
                    else:
                        blk.append(row0 // CHUNK + (p * nb + t) * 2 + q)
                        sidx.append(s0 + 2 * p + q)
                    flags.append(int(t == 0) + 2 * int(t == nb - 1))
        row0 += bsz * t_len
        s0 += bsz
    return tuple(jnp.asarray(np.asarray(a, np.int32)) for a in (blk, sidx, flags))


RW_G = 2 * RWKV_HEADS
RW_PAIRS = RW_G // 2
RW_UNROLL = 8


def _halves(x, lo):
    sa = jnp.sum(jnp.where(lo, x, 0.0), axis=1, keepdims=True)
    sb = jnp.sum(jnp.where(lo, 0.0, x), axis=1, keepdims=True)
    return jnp.where(lo, sa, sb)


def _rwkv_kernel(blk_ref, sidx_ref, flag_ref,
                 pr_ref, sh_ref, s0_ref,
                 mu_ref, w0_ref, wup_ref, a0_ref, aup_ref, gup_ref,
                 kk_ref, ka_ref, rk_ref, gnw_ref, gnb_ref, onehot_ref,
                 o_ref, st_ref, sho_ref, *scr):
    st = scr[:RW_PAIRS]
    prev_scr, ops_scr, post_scr, vt_scr, vb_scr, oblk_scr = scr[RW_PAIRS:]
    i = pl.program_id(0)
    flag = flag_ref[i]
    tt = CHUNK
    rows = 2 * tt
    hd = RWKV_HD

    @pl.when((flag & 1) == 1)
    def _():
        for q in range(2):
            for p in range(RWKV_HEADS // 2):
                st[q * 4 + p][...] = jnp.concatenate([s0_ref[q, 2 * p], s0_ref[q, 2 * p + 1]], axis=1)
            prev_scr[q] = sh_ref[q]

    rowi = lax.broadcasted_iota(jnp.int32, (rows, 1), 0)
    lo_rows = lax.broadcasted_iota(jnp.int32, (rows, LANES), 1) < hd

    x = pr_ref[...]
    xprev = jnp.where(rowi == 0, prev_scr[0],
                      jnp.where(rowi == tt, prev_scr[1], pltpu.roll(x, 1, 0)))
    prev_scr[0] = x[tt - 1:tt, :]
    prev_scr[1] = x[rows - 1:rows, :]
    xs = x + (xprev - x) * mu_ref[...]
    r = xs[:, 0:RWKV_DIM]
    k = xs[:, RWKV_DIM:2 * RWKV_DIM]
    v = xs[:, 2 * RWKV_DIM:3 * RWKV_DIM]
    lora = xs[:, 3 * RWKV_DIM:3 * RWKV_DIM + DECAY_LORA + AAA_LORA]
    gd = xs[:, 3 * RWKV_DIM + DECAY_LORA + AAA_LORA:]
    w = jnp.exp(-DECAY_SCALE * _sigmoid(w0_ref[...] + _dot(jnp.tanh(lora), wup_ref[...])))
    a = _sigmoid(a0_ref[...] + _dot(lora, aup_ref[...]))
    g = _dot(_sigmoid(gd), gup_ref[...])
    kkraw = k * kk_ref[...]
    kt = k * (1.0 + (a - 1.0) * ka_ref[...])
    prod = r * kt * rk_ref[...]
    bsums = []
    zpad = jnp.zeros((tt, LANES), F32)
    for p in range(RWKV_HEADS // 2):
        cs = slice(LANES * p, LANES * (p + 1))
        kk_p = kkraw[:, cs]
        kk_p = kk_p / jnp.maximum(jnp.sqrt(_halves(kk_p * kk_p, lo_rows)), 1e-12)
        kka_p = kk_p * a[:, cs]
        bsums.append(_halves(prod[:, cs], lo_rows))
        for q in range(2):
            pp = q * 4 + p
            rs = slice(q * tt, (q + 1) * tt)
            ops_scr[0, pp] = r[rs, cs]
            ops_scr[1, pp] = w[rs, cs]
            ops_scr[2, pp] = kt[rs, cs]
            ops_scr[3, pp] = kk_p[rs].astype(BF16).astype(F32)
            ops_scr[4, pp] = kka_p[rs]
            vt = jnp.concatenate([v[rs, cs], zpad], axis=0).T[:, :tt]
            hi = vt.astype(BF16).astype(F32)
            low = vt - hi
            vt_scr[pp * hd:(pp + 1) * hd, :] = jnp.concatenate(
                [hi[:hd], low[:hd], hi[hd:], low[hd:]], axis=1).astype(BF16)
    post_scr[0] = g
    post_scr[1] = jnp.concatenate(bsums, axis=1) * v

    lo = lax.broadcasted_iota(jnp.int32, (hd, LANES), 1) < hd
    lo8 = lax.broadcasted_iota(jnp.int32, (8, LANES), 1) < hd
    row8 = lax.broadcasted_iota(jnp.int32, (8, LANES), 0)
    for c in range(tt // 8):
        cs = slice(8 * LANES * c, 8 * LANES * (c + 1))
        vb_scr[:, cs] = jnp.dot(vt_scr[...], onehot_ref[:, cs], preferred_element_type=F32)

    def step_group(tg, carry):
        t0 = tg * RW_UNROLL
        states = [st[pp][...] for pp in range(RW_PAIRS)]
        rounded = [s.astype(BF16) for s in states]
        for j in range(RW_UNROLL):
            t = t0 + j
            for pp in range(RW_PAIRS):
                s = states[pp]
                r_t = ops_scr[0, pp, pl.ds(t, 1), :]
                w_t = ops_scr[1, pp, pl.ds(t, 1), :]
                k_t = ops_scr[2, pp, pl.ds(t, 1), :]
                kk_t = ops_scr[3, pp, pl.ds(t, 1), :]
                kka_t = ops_scr[4, pp, pl.ds(t, 1), :]
                sa = _halves(rounded[pp].astype(F32) * kk_t, lo)
                v_t = vb_scr[pp * hd:(pp + 1) * hd, pl.ds(pl.multiple_of(t * LANES, LANES), LANES)]
                s = s * w_t - sa * kka_t + v_t * k_t
                states[pp] = s
                rounded[pp] = s.astype(BF16)
                r2 = jnp.where(row8 == 0, jnp.where(lo8, r_t, 0.0),
                               jnp.where(row8 == 1, jnp.where(lo8, 0.0, r_t), 0.0))
                o2 = _dot_nt(r2, rounded[pp])
                oblk_scr[2 * pp, pl.ds(t, 1), :] = o2[0:1]
                oblk_scr[2 * pp + 1, pl.ds(t, 1), :] = o2[1:2]
        for pp in range(RW_PAIRS):
            st[pp][...] = states[pp]
        return carry

    lax.fori_loop(0, tt // RW_UNROLL, step_group, 0)

    def group_norm(o):
        mu = jnp.mean(o, axis=1, keepdims=True)
        d = o - mu
        var = jnp.mean(d * d, axis=1, keepdims=True)
        return d * lax.rsqrt(var + RWKV_GN_EPS)

    on = jnp.concatenate(
        [jnp.concatenate([group_norm(oblk_scr[q * RWKV_HEADS + h]) for h in range(RWKV_HEADS)], axis=1)
         for q in range(2)], axis=0)
    out = (on * gnw_ref[...] + gnb_ref[...] + post_scr[1]) * post_scr[0]
    o_ref[...] = out.astype(o_ref.dtype)

    @pl.when((flag & 2) == 2)
    def _():
        for q in range(2):
            for p in range(RWKV_HEADS // 2):
                s = st[q * 4 + p][...]
                st_ref[q, 2 * p] = s[:, :hd]
                st_ref[q, 2 * p + 1] = s[:, hd:]
            sho_ref[q] = prev_scr[q]


def _value_onehot():
    r = np.arange(4 * CHUNK)[:, None]
    c = np.arange(CHUNK * LANES)[None, :]
    hit = ((r % CHUNK) == (c // LANES)) & ((r // (2 * CHUNK)) == ((c % LANES) // RWKV_HD))
    return jnp.asarray(hit.astype(np.float32), BF16)


def _rwkv_mix(proj, groups, shift0, s0, params):
    rows = proj.shape[0]
    n_seq = s0.shape[0]
    blk, sidx, flags = _seq_items(groups, 2)
    full = lambda a: pl.BlockSpec(a.shape, lambda i, b, s, f: (0,) * a.ndim)
    st_spec = pl.BlockSpec((2, RWKV_HEADS, RWKV_HD, RWKV_HD), lambda i, b, s, f: (s[i], 0, 0, 0))
    sh_spec = pl.BlockSpec((2, 1, RWKV_PROJ), lambda i, b, s, f: (s[i], 0, 0))
    grid_spec = pltpu.PrefetchScalarGridSpec(
        num_scalar_prefetch=3,
        grid=(flags.shape[0],),
        in_specs=[pl.BlockSpec((2 * CHUNK, RWKV_PROJ), lambda i, b, s, f: (b[i], 0)),
                  sh_spec, st_spec] + [full(a) for a in params]
                 + [pl.BlockSpec((4 * CHUNK, CHUNK * LANES), lambda i, b, s, f: (0, 0),
                                 pipeline_mode=pl.Buffered(1))],
        out_specs=[pl.BlockSpec((2 * CHUNK, RWKV_DIM), lambda i, b, s, f: (b[i], 0)), st_spec, sh_spec],
        scratch_shapes=[pltpu.VMEM((RWKV_HD, LANES), F32) for _ in range(RW_PAIRS)]
                       + [pltpu.VMEM((2, 1, RWKV_PROJ), F32),
                          pltpu.VMEM((5, RW_PAIRS, CHUNK, LANES), F32),
                          pltpu.VMEM((2, 2 * CHUNK, RWKV_DIM), F32),
                          pltpu.VMEM((RW_PAIRS * RWKV_HD, 4 * CHUNK), BF16),
                          pltpu.VMEM((RW_PAIRS * RWKV_HD, CHUNK * LANES), F32),
                          pltpu.VMEM((RW_G, CHUNK, RWKV_HD), F32)],
    )
    return pl.pallas_call(
        _rwkv_kernel,
        grid_spec=grid_spec,
        out_shape=[jax.ShapeDtypeStruct((rows, RWKV_DIM), BF16),
                   jax.ShapeDtypeStruct((n_seq, RWKV_HEADS, RWKV_HD, RWKV_HD), F32),
                   jax.ShapeDtypeStruct((n_seq, 1, RWKV_PROJ), F32)],
        compiler_params=_cparams("arbitrary"),
        name="rwkv_mix",
    )(blk, sidx, flags, proj, shift0, s0, *params, _value_onehot())


SUB = 16
LEVELS = (16, 8, 4, 2)
G_PIECES = 2
EX_B, EX_TAIL, EX_Q, EX_K1 = 0, 1, 2, 3
EX_LVL = EX_K1 + CHUNK // SUB - 1
EX_BLOCKS = EX_LVL + len(LEVELS)


def _chunk_tables():
    r = np.arange(CHUNK)[:, None]
    c = np.arange(CHUNK)[None, :]
    mats = [c <= r, c > r, (c >= (r // SUB) * SUB) & (c <= r)]
    for i in range(1, CHUNK // SUB):
        mats.append((c > r) & (c <= SUB * i - 1))
    masks = []
    for size in LEVELS:
        half = size // 2
        mid = (r // size) * size + half - 1
        upper = (r % size) >= half
        mats.append(np.where(upper, (c > mid) & (c <= r), (c > r) & (c <= mid)))
        masks.append(((r // size) == (c // size)) & upper & ((c % size) < half))
    e = np.concatenate(mats, axis=0).astype(np.float32)
    return (jnp.asarray(np.concatenate([e] * G_PIECES, axis=1), BF16),
            jnp.asarray(np.stack(masks).astype(np.float32)))


def _chunk_consts(dk):
    row = lax.broadcasted_iota(jnp.int32, (CHUNK, 1), 0)
    r = lax.broadcasted_iota(jnp.int32, (CHUNK, CHUNK), 0)
    c = lax.broadcasted_iota(jnp.int32, (CHUNK, CHUNK), 1)
    return dict(
        upper=[(row % size) >= size // 2 for size in LEVELS],
        blkrow=row // SUB,
        cross=(r // SUB) > (c // SUB),
        eye_t=r == c,
        eye=(lax.broadcasted_iota(jnp.int32, (dk, dk), 0)
             == lax.broadcasted_iota(jnp.int32, (dk, dk), 1)),
    )


def _split_bf16(x, pieces):
    out = []
    for _ in range(pieces - 1):
        p = x.astype(BF16)
        out.append(p)
        x = x - p.astype(F32)
    out.append(x.astype(BF16))
    return out


def _chunk_core(qs, ks, vs, gs, ss, ecat_ref, lmask_ref, cst):
    n = len(qs)
    ecat = ecat_ref[...]
    exs = [jnp.exp(jnp.dot(ecat, jnp.concatenate(_split_bf16(g, G_PIECES), axis=0),
                           preferred_element_type=F32)) for g in gs]
    blk = lambda ex, i: ex[i * CHUNK:(i + 1) * CHUNK]
    o_inter = [_dot(qs[h] * blk(exs[h], EX_B), ss[h]) for h in range(n)]
    p_cross = []
    for h in range(n):
        qg = qs[h] * blk(exs[h], EX_Q)
        rows = [jnp.zeros((SUB, CHUNK), F32)]
        for i in range(1, CHUNK // SUB):
            rows.append(_dot_nt(qg[SUB * i:SUB * (i + 1)], ks[h] * blk(exs[h], EX_K1 + i - 1)))
        p_cross.append(jnp.concatenate(rows, axis=0))
    p_diag = []
    for h in range(n):
        acc = jnp.zeros((CHUNK, CHUNK), F32)
        for lvl in range(len(LEVELS)):
            a = jnp.where(cst["upper"][lvl], qs[h], ks[h]) * blk(exs[h], EX_LVL + lvl)
            a_hi, a_lo = _split_bf16(a, 2)
            acc = acc + lmask_ref[lvl] * lax.dot_general(
                jnp.concatenate([a_hi, a_hi, a_lo], axis=1), jnp.concatenate([a_hi, a_lo, a_hi], axis=1),
                (((1,), (1,)), ((), ())), preferred_element_type=F32)
        p_diag.append(jnp.where(cst["eye_t"], jnp.sum(qs[h] * ks[h], axis=1, keepdims=True), acc))
    os_ = [o_inter[h] + _dot(jnp.where(cst["cross"], p_cross[h], p_diag[h]), vs[h]) for h in range(n)]
    s_new = []
    for h in range(n):
        dcol = jnp.sum(jnp.where(cst["eye"], exs[h][CHUNK - 1:CHUNK], 0.0), axis=1, keepdims=True)
        s_new.append(ss[h] * dcol + _dot_tn(ks[h] * blk(exs[h], EX_TAIL), vs[h]))
    return os_, s_new


def _rms_norm(o, w):
    return o * lax.rsqrt(jnp.mean(o * o, axis=-1, keepdims=True) + NORM_EPS) * w


def _log_sigmoid(x):
    return jnp.minimum(x, 0.0) - jnp.log1p(jnp.exp(-jnp.abs(x)))


GLA_DKP = 128
GLA_MAIN = 4 * GLA_HEADS * 128


def _gla_kernel(blk_ref, sidx_ref, flag_ref,
                gl_ref, gz_ref, s0_ref, gup_ref, gb_ref, nw_ref, ecat_ref, lmask_ref,
                o_ref, st_ref, *st):
    i = pl.program_id(0)
    flag = flag_ref[i]
    hw = GLA_HEADS * GLA_DKP

    @pl.when((flag & 1) == 1)
    def _():
        for q in range(2):
            for h in range(GLA_HEADS):
                st[q * GLA_HEADS + h][...] = s0_ref[q, h]

    cst = _chunk_consts(GLA_DKP)
    log_a = _log_sigmoid(_dot(gz_ref[...], gup_ref[...]) + gb_ref[...]) / GLA_TAU
    qs, ks, vs, gs = [], [], [], []
    for q in range(2):
        rs = slice(q * CHUNK, (q + 1) * CHUNK)
        for h in range(GLA_HEADS):
            qs.append(gl_ref[rs, GLA_DKP * h:GLA_DKP * (h + 1)] * GLA_DK ** -0.5)
            ks.append(gl_ref[rs, hw + GLA_DKP * h:hw + GLA_DKP * (h + 1)])
            vs.append(gl_ref[rs, 2 * hw + GLA_DV * h:2 * hw + GLA_DV * (h + 1)])
            gs.append(log_a[rs, GLA_DKP * h:GLA_DKP * (h + 1)])
    os_, s_new = _chunk_core(qs, ks, vs, gs, [r[...] for r in st], ecat_ref, lmask_ref, cst)
    for r, s in zip(st, s_new):
        r[...] = s
    normed = [_rms_norm(o, nw_ref[...]) for o in os_]
    on = jnp.concatenate([jnp.concatenate(normed[q * GLA_HEADS:(q + 1) * GLA_HEADS], axis=1)
                          for q in range(2)], axis=0)
    og = gl_ref[:, 3 * hw:3 * hw + GLA_VDIM]
    o_ref[...] = (on * (og * _sigmoid(og))).astype(o_ref.dtype)

    @pl.when((flag & 2) == 2)
    def _():
        for q in range(2):
            for h in range(GLA_HEADS):
                st_ref[q, h] = st[q * GLA_HEADS + h][...]


def _gla_mix(proj, groups, s0, gup, gb, nw, tables):
    rows = proj.shape[0]
    blk, sidx, flags = _seq_items(groups, 2)
    full = lambda a: pl.BlockSpec(a.shape, lambda i, b, s, f: (0,) * a.ndim)
    st_spec = pl.BlockSpec((2, GLA_HEADS, GLA_DKP, GLA_DV), lambda i, b, s, f: (s[i], 0, 0, 0))
    grid_spec = pltpu.PrefetchScalarGridSpec(
        num_scalar_prefetch=3,
        grid=(flags.shape[0],),
        in_specs=[pl.BlockSpec((2 * CHUNK, GLA_MAIN), lambda i, b, s, f: (b[i], 1)),
                  pl.BlockSpec((2 * CHUNK, GLA_GZW), lambda i, b, s, f: (b[i], RWKV_PROJ // GLA_GZW)),
                  st_spec, full(gup), full(gb), full(nw), full(tables[0]), full(tables[1])],
        out_specs=[pl.BlockSpec((2 * CHUNK, GLA_VDIM), lambda i, b, s, f: (b[i], 0)), st_spec],
        scratch_shapes=[pltpu.VMEM((GLA_DKP, GLA_DV), F32) for _ in range(2 * GLA_HEADS)],
    )
    return pl.pallas_call(
        _gla_kernel,
        grid_spec=grid_spec,
        out_shape=[jax.ShapeDtypeStruct((rows, GLA_VDIM), BF16),
                   jax.ShapeDtypeStruct(s0.shape, F32)],
        compiler_params=_cparams("arbitrary"),
        name="gla_mix",
    )(blk, sidx, flags, proj, proj, s0, gup, gb, nw, *tables)


def _hgrn_kernel(blk_ref, sidx_ref, flag_ref,
                 pc_ref, s0_ref, lb_ref, nw_ref, ecat_ref, lmask_ref,
                 o_ref, st_ref, *st):
    i = pl.program_id(0)
    flag = flag_ref[i]

    @pl.when((flag & 1) == 1)
    def _():
        for h in range(HGRN_HEADS):
            st[h][...] = s0_ref[0, h]

    cst = _chunk_consts(HGRN_HD)
    qs, ks, vs, gs = [], [], [], []
    for h in range(HGRN_HEADS):
        hs = slice(HGRN_HD * h, HGRN_HD * (h + 1))
        qx = pc_ref[:, HGRN_HD * h:HGRN_HD * (h + 1)]
        fx = pc_ref[:, HGRN_DIM + HGRN_HD * h:HGRN_DIM + HGRN_HD * (h + 1)]
        lb = lb_ref[:, hs]
        qs.append(qx * _sigmoid(qx))
        gs.append(jnp.log(lb + (1.0 - lb) * _sigmoid(fx)))
        ks.append((1.0 - lb) * _sigmoid(-fx))
        vs.append(pc_ref[:, 2 * HGRN_DIM + HGRN_HD * h:2 * HGRN_DIM + HGRN_HD * (h + 1)])
    os_, s_new = _chunk_core(qs, ks, vs, gs, [r[...] for r in st], ecat_ref, lmask_ref, cst)
    for r, s in zip(st, s_new):
        r[...] = s
    on = jnp.concatenate([_rms_norm(o, nw_ref[...]) for o in os_], axis=1)
    gx = pc_ref[:, 3 * HGRN_DIM:4 * HGRN_DIM]
    o_ref[...] = (on * (gx * _sigmoid(gx))).astype(o_ref.dtype)

    @pl.when((flag & 2) == 2)
    def _():
        for h in range(HGRN_HEADS):
            st_ref[0, h] = st[h][...]


def _hgrn_mix(proj, groups, s0, lb, nw, tables):
    rows = proj.shape[0]
    blk, sidx, flags = _seq_items(groups, 1)
    full = lambda a: pl.BlockSpec(a.shape, lambda i, b, s, f: (0,) * a.ndim)
    st_spec = pl.BlockSpec((1, HGRN_HEADS, HGRN_HD, HGRN_HD), lambda i, b, s, f: (s[i], 0, 0, 0))
    grid_spec = pltpu.PrefetchScalarGridSpec(
        num_scalar_prefetch=3,
        grid=(flags.shape[0],),
        in_specs=[pl.BlockSpec((CHUNK, 4 * HGRN_DIM), lambda i, b, s, f: (b[i], 0)),
                  st_spec, full(lb), full(nw), full(tables[0]), full(tables[1])],
        out_specs=[pl.BlockSpec((CHUNK, HGRN_DIM), lambda i, b, s, f: (b[i], 0)), st_spec],
        scratch_shapes=[pltpu.VMEM((HGRN_HD, HGRN_HD), F32) for _ in range(HGRN_HEADS)],
    )
    return pl.pallas_call(
        _hgrn_kernel,
        grid_spec=grid_spec,
        out_shape=[jax.ShapeDtypeStruct((rows, HGRN_DIM), BF16),
                   jax.ShapeDtypeStruct(s0.shape, F32)],
        compiler_params=_cparams("arbitrary"),
        name="hgrn_mix",
    )(blk, sidx, flags, proj, s0, lb, nw, *tables)


ROUTER_TM = 1024
MOE_TQ = 512
DMA_UNROLL = 8
META_E1, META_E2, META_P1, META_P2, META_R1, META_R2 = range(6)


def _router_kernel(x_ref, wr_ref, meta_ref, cnt_ref, carry_scr):
    i = pl.program_id(0)

    @pl.when(i == 0)
    def _():
        carry_scr[...] = jnp.zeros_like(carry_scr)

    tm = x_ref.shape[0]
    logits = _dot(x_ref[...], wr_ref[...])
    lane = lax.broadcasted_iota(jnp.int32, (tm, LANES), 1)
    valid = lane < N_EXPERTS
    lg = jnp.where(valid, logits, -1e30)
    ex = jnp.where(valid, jnp.exp(lg - jnp.max(lg, axis=-1, keepdims=True)), 0.0)
    probs = ex / jnp.sum(ex, axis=-1, keepdims=True)
    p1 = jnp.max(probs, axis=-1, keepdims=True)
    i1 = jnp.min(jnp.where(jnp.logical_and(probs == p1, valid), lane, LANES), axis=-1, keepdims=True)
    rest = jnp.where(jnp.logical_or(lane == i1, jnp.logical_not(valid)), -1.0, probs)
    p2 = jnp.max(rest, axis=-1, keepdims=True)
    i2 = jnp.min(jnp.where(rest == p2, lane, LANES), axis=-1, keepdims=True)
    den = p1 + p2
    oh1 = (lane == i1).astype(F32)
    oh2 = (lane == i2).astype(F32)
    both = oh1 + oh2
    rr = lax.broadcasted_iota(jnp.int32, (tm, tm), 0)
    cc = lax.broadcasted_iota(jnp.int32, (tm, tm), 1)
    before = _dot((rr > cc).astype(F32), both) + carry_scr[...]
    r1 = jnp.sum(oh1 * before, axis=-1, keepdims=True)
    r2 = jnp.sum(oh2 * before, axis=-1, keepdims=True)
    carry_scr[...] += jnp.sum(both, axis=0, keepdims=True)
    meta = jnp.zeros((tm, LANES), F32)
    for idx, col in ((META_E1, i1.astype(F32)), (META_E2, i2.astype(F32)),
                     (META_P1, p1 / den), (META_P2, p2 / den), (META_R1, r1), (META_R2, r2)):
        meta = jnp.where(lane == idx, col, meta)
    meta_ref[...] = meta
    cnt_ref[...] = jnp.broadcast_to(carry_scr[...], cnt_ref.shape)


def _router(x, wr, tm):
    m, d = x.shape
    return pl.pallas_call(
        _router_kernel,
        grid=(m // tm,),
        in_specs=[pl.BlockSpec((tm, d), lambda i: (i, 0)),
                  pl.BlockSpec(wr.shape, lambda i: (0, 0))],
        out_specs=[pl.BlockSpec((tm, LANES), lambda i: (i, 0)),
                   pl.BlockSpec((8, LANES), lambda i: (0, 0))],
        out_shape=[jax.ShapeDtypeStruct((m, LANES), F32),
                   jax.ShapeDtypeStruct((8, LANES), F32)],
        scratch_shapes=[pltpu.VMEM((1, LANES), F32)],
        compiler_params=_cparams("arbitrary"),
        name="moe_router",
    )(x, wr)


def _scatter_kernel(dest_ref, x_ref, xs_in_ref, xs_ref, idx_smem, sem_idx, sem):
    del xs_in_ref
    i = pl.program_id(0)
    tq = x_ref.shape[0]
    cp = pltpu.make_async_copy(dest_ref.at[i], idx_smem, sem_idx)
    cp.start()
    cp.wait()

    def row_copy(t, slot):
        return pltpu.make_async_copy(x_ref.at[pl.ds(t, 1)],
                                     xs_ref.at[pl.ds(idx_smem[slot * tq + t], 1)], sem)

    def issue(t, carry):
        row_copy(t, 0).start()
        row_copy(t, 1).start()
        return carry

    def drain(t, carry):
        row_copy(t, 0).wait()
        row_copy(t, 1).wait()
        return carry

    lax.fori_loop(0, tq, issue, 0, unroll=DMA_UNROLL)
    lax.fori_loop(0, tq, drain, 0, unroll=DMA_UNROLL)


def _moe_scatter(x, dest, n_rows, tq):
    m, d = x.shape
    zeros = jnp.zeros((n_rows, d), x.dtype)
    any_spec = pl.BlockSpec(memory_space=pl.ANY)
    return pl.pallas_call(
        _scatter_kernel,
        grid=(m // tq,),
        in_specs=[any_spec, pl.BlockSpec((tq, d), lambda i: (i, 0)), any_spec],
        out_specs=any_spec,
        out_shape=jax.ShapeDtypeStruct((n_rows, d), x.dtype),
        scratch_shapes=[pltpu.SMEM((2 * tq,), jnp.int32),
                        pltpu.SemaphoreType.DMA(()),
                        pltpu.SemaphoreType.DMA(())],
        input_output_aliases={2: 0},
        compiler_params=pltpu.CompilerParams(dimension_semantics=("arbitrary",),
                                             has_side_effects=True),
        name="moe_scatter",
    )(dest, x, zeros)


def _expert_kernel(be_ref, bv_ref, x_ref, w1_ref, w3_ref, w2_ref, y_ref, acc_ref):
    b = pl.program_id(0)

    @pl.when(bv_ref[b] == 1)
    def _():
        _swiglu_acc(x_ref[...].astype(BF16), w1_ref, w3_ref, w2_ref, acc_ref)
        y_ref[...] = acc_ref[...]

    @pl.when(bv_ref[b] == 0)
    def _():
        y_ref[...] = jnp.zeros_like(y_ref)


def _moe_experts(xs, block_expert, block_valid, j, w1, w3, w2):
    rows, d = xs.shape
    wspec = lambda w: pl.BlockSpec((None, None) + w.shape[2:], lambda b, be, bv: (j, be[b], 0, 0),
                                   pipeline_mode=pl.Buffered(1))
    grid_spec = pltpu.PrefetchScalarGridSpec(
        num_scalar_prefetch=2,
        grid=(rows // MOE_BLK,),
        in_specs=[pl.BlockSpec((MOE_BLK, d), lambda b, be, bv: (b, 0)),
                  wspec(w1), wspec(w3), wspec(w2)],
        out_specs=pl.BlockSpec((MOE_BLK, d), lambda b, be, bv: (b, 0)),
        scratch_shapes=[pltpu.VMEM((MOE_BLK, d), F32)],
    )
    return pl.pallas_call(
        _expert_kernel,
        grid_spec=grid_spec,
        out_shape=jax.ShapeDtypeStruct((rows, d), F32),
        compiler_params=_cparams("arbitrary"),
        name="moe_experts",
    )(block_expert, block_valid, xs, w1, w3, w2)


def _combine_kernel(dest_ref, y_ref, x_ref, meta_ref, g_ref, b_ref, o_ref,
                    idx_smem, ya_scr, yb_scr, sem_idx, sem):
    i = pl.program_id(0)
    tq = x_ref.shape[0]
    cp = pltpu.make_async_copy(dest_ref.at[i], idx_smem, sem_idx)
    cp.start()
    cp.wait()

    def row_copy(t, slot, buf):
        return pltpu.make_async_copy(y_ref.at[pl.ds(idx_smem[slot * tq + t], 1)],
                                     buf.at[pl.ds(t, 1)], sem)

    def issue(t, carry):
        row_copy(t, 0, ya_scr).start()
        row_copy(t, 1, yb_scr).start()
        return carry

    def drain(t, carry):
        row_copy(t, 0, ya_scr).wait()
        row_copy(t, 1, yb_scr).wait()
        return carry

    lax.fori_loop(0, tq, issue, 0, unroll=DMA_UNROLL)
    lax.fori_loop(0, tq, drain, 0, unroll=DMA_UNROLL)
    meta = meta_ref[...]
    moe = ya_scr[...] * meta[:, META_P1:META_P1 + 1] + yb_scr[...] * meta[:, META_P2:META_P2 + 1]
    o_ref[...] = _layer_norm(ALPHA * x_ref[...] + moe, g_ref[...], b_ref[...])


def _moe_combine(dest, y, x, meta, g, b, tq):
    m, d = x.shape
    any_spec = pl.BlockSpec(memory_space=pl.ANY)
    const = lambda i: (0, 0)
    return pl.pallas_call(
        _combine_kernel,
        grid=(m // tq,),
        in_specs=[any_spec, any_spec,
                  pl.BlockSpec((tq, d), lambda i: (i, 0)),
                  pl.BlockSpec((tq, LANES), lambda i: (i, 0)),
                  pl.BlockSpec((1, d), const), pl.BlockSpec((1, d), const)],
        out_specs=pl.BlockSpec((tq, d), lambda i: (i, 0)),
        out_shape=jax.ShapeDtypeStruct((m, d), F32),
        scratch_shapes=[pltpu.SMEM((2 * tq,), jnp.int32),
                        pltpu.VMEM((tq, d), F32), pltpu.VMEM((tq, d), F32),
                        pltpu.SemaphoreType.DMA(()),
                        pltpu.SemaphoreType.DMA(())],
        compiler_params=_cparams("arbitrary"),
        name="moe_combine",
    )(dest, y, x, meta, g.reshape(1, d), b.reshape(1, d))


def _moe_ln(x, router, j, w1, w3, w2, g, b):
    m, d = x.shape
    tq = _tile(m, MOE_TQ)
    wr = jnp.pad(router, ((0, 0), (0, LANES - N_EXPERTS)))
    meta, cnt = _router(x, wr, _tile(m, ROUTER_TM))
    counts = cnt[0, :N_EXPERTS].astype(jnp.int32)
    padded = (counts + MOE_BLK - 1) // MOE_BLK * MOE_BLK
    pend = jnp.cumsum(padded)
    pstart = pend - padded
    n_blocks = (2 * m) // MOE_BLK + N_EXPERTS
    e1 = meta[:, META_E1].astype(jnp.int32)
    e2 = meta[:, META_E2].astype(jnp.int32)
    onehot = lambda e: (e[:, None] == jnp.arange(N_EXPERTS)[None, :]).astype(jnp.int32)
    dest1 = jnp.sum(onehot(e1) * pstart[None, :], axis=1) + meta[:, META_R1].astype(jnp.int32)
    dest2 = jnp.sum(onehot(e2) * pstart[None, :], axis=1) + meta[:, META_R2].astype(jnp.int32)
    dest = jnp.concatenate([dest1.reshape(m // tq, tq), dest2.reshape(m // tq, tq)], axis=1)
    block_start = jnp.arange(n_blocks, dtype=jnp.int32) * MOE_BLK
    block_expert = jnp.minimum(jnp.sum((block_start[:, None] >= pend[None, :]).astype(jnp.int32), axis=1),
                               N_EXPERTS - 1).astype(jnp.int32)
    block_valid = (block_start < pend[-1]).astype(jnp.int32)
    xs = _moe_scatter(x, dest, n_blocks * MOE_BLK, tq)
    y = _moe_experts(xs, block_expert, block_valid, j, w1, w3, w2)
    return _moe_combine(dest, y, x, meta, g, b, tq)


def _even_layer(x, j, groups, shift0, s_rwkv, s_gla, tables, p):
    m = x.shape[0]
    w_in = p["w_in_ab"][j]
    pg = w_in[:, RWKV_PROJ:]
    zcol = lambda n: jnp.zeros((D_MODEL, n), F32)

    def pad_heads(wq):
        wq = wq.reshape(D_MODEL, GLA_HEADS, GLA_DK)
        return jnp.pad(wq, ((0, 0), (0, 0), (0, GLA_DKP - GLA_DK))).reshape(D_MODEL, GLA_HEADS * GLA_DKP)

    gz0 = 2 * GLA_KDIM + GLA_VDIM
    packed = jnp.concatenate(
        [w_in[:, :RWKV_PROJ],
         pg[:, gz0:gz0 + GLA_RANK], zcol(2 * GLA_GZW - GLA_RANK),
         pad_heads(pg[:, 0:GLA_KDIM]), pad_heads(pg[:, GLA_KDIM:2 * GLA_KDIM]),
         pg[:, 2 * GLA_KDIM:gz0],
         pg[:, gz0 + GLA_RANK:]], axis=1).astype(BF16)
    proj = _matmul(x, packed, _tile(m, 1024), 2048)
    zl = jnp.zeros((DECAY_LORA, RWKV_DIM), F32)
    row = lambda a: a.reshape(1, -1)
    rw_params = [row(p["rwkv_mu"][j]), row(p["rwkv_w0"][j]),
                 jnp.concatenate([p["rwkv_w_up"][j], zl], axis=0).astype(BF16),
                 row(p["rwkv_a0"][j]),
                 jnp.concatenate([zl, p["rwkv_a_up"][j]], axis=0).astype(BF16),
                 p["rwkv_g_up"][j].astype(BF16),
                 row(p["rwkv_k_k"][j]), row(p["rwkv_k_a"][j]), row(p["rwkv_r_k"][j]),
                 row(p["rwkv_gn_w"][j]), row(p["rwkv_gn_b"][j])]
    o_rwkv, s_rwkv, shift_new = _rwkv_mix(proj, groups, shift0, s_rwkv, rw_params)
    pad_k = lambda a: jnp.pad(a.reshape(-1, GLA_HEADS, GLA_DK),
                              ((0, 0), (0, 0), (0, GLA_DKP - GLA_DK))).reshape(-1, GLA_HEADS * GLA_DKP)
    gup = jnp.pad(pad_k(p["gla_gate_up"][j]), ((0, GLA_GZW - GLA_RANK), (0, 0))).astype(BF16)
    s_gla_p = jnp.pad(s_gla, ((0, 0), (0, 0), (0, GLA_DKP - GLA_DK), (0, 0)))
    o_gla, s_gla_p = _gla_mix(proj, groups, s_gla_p, gup, pad_k(row(p["gla_gate_b"][j])),
                              row(p["gla_norm_w"][j]), tables)
    w_out = p["w_out_ab"][j].astype(BF16)
    x = _proj_ln([o_rwkv, o_gla], [w_out[:RWKV_DIM], w_out[RWKV_DIM:]], x,
                 p["ln_ab_g"][j], p["ln_ab_b"][j], _tile(m, 1024))
    x = _ffn_ln(x, p["ffn_w1"][j].astype(BF16), p["ffn_w3"][j].astype(BF16), p["ffn_w2"][j].astype(BF16),
                p["ln_ffn_g"][j], p["ln_ffn_b"][j], _tile(m, 1024))
    return x, shift_new[:, 0, :], s_rwkv, s_gla_p[:, :, :GLA_DK, :]


def _odd_layer(x, j, groups, lb, s_hgrn, tables, p):
    m = x.shape[0]
    proj = _matmul(x, p["w_in_c"][j].astype(BF16), _tile(m, 1024), 2048)
    o, s_hgrn = _hgrn_mix(proj, groups, s_hgrn, lb.reshape(1, -1), p["hgrn_norm_w"][j].reshape(1, -1), tables)
    x = _proj_ln([o], [p["w_out_c"][j].astype(BF16)], x, p["ln_c_g"][j], p["ln_c_b"][j], _tile(m, 1024))
    x = _moe_ln(x, p["moe_router"][j], j, p["moe_w1"], p["moe_w3"], p["moe_w2"],
                p["ln_moe_g"][j], p["ln_moe_b"][j])
    return x, s_hgrn


def _trunk(x, groups, s_rwkv, s_shift, s_gla, s_hgrn, lower_bounds, p):
    new_rwkv, new_shift, new_gla, new_hgrn = [], [], [], []
    tables = _chunk_tables()
    for layer in range(DEPTH):
        j = layer // 2
        if layer % 2 == 0:
            x, sh, sr, sg = _even_layer(x, j, groups, s_shift[j][:, None, :], s_rwkv[j], s_gla[j], tables, p)
            new_rwkv.append(sr)
            new_shift.append(sh)
            new_gla.append(sg)
        else:
            x, sc = _odd_layer(x, j, groups, lower_bounds[layer], s_hgrn[j], tables, p)
            new_hgrn.append(sc)
    return x, jnp.stack(new_rwkv), jnp.stack(new_shift), jnp.stack(new_gla), jnp.stack(new_hgrn)


def kernel(x_prompt, x_sample, state_rwkv, cache_rwkv_shift, state_gla, state_hgrn,
           w_in_ab, rwkv_mu, rwkv_w0, rwkv_w_up, rwkv_a0, rwkv_a_up, rwkv_g_up, rwkv_k_k, rwkv_k_a,
           rwkv_r_k, rwkv_gn_w, rwkv_gn_b, gla_gate_up, gla_gate_b, gla_norm_w, w_out_ab, ln_ab_g, ln_ab_b,
           ffn_w1, ffn_w3, ffn_w2, ln_ffn_g, ln_ffn_b,
           w_in_c, hgrn_lb, hgrn_norm_w, w_out_c, ln_c_g, ln_c_b,
           moe_router, moe_w1, moe_w3, moe_w2, ln_moe_g, ln_moe_b):
    p = dict(w_in_ab=w_in_ab, rwkv_mu=rwkv_mu, rwkv_w0=rwkv_w0, rwkv_w_up=rwkv_w_up, rwkv_a0=rwkv_a0,
             rwkv_a_up=rwkv_a_up, rwkv_g_up=rwkv_g_up, rwkv_k_k=rwkv_k_k, rwkv_k_a=rwkv_k_a,
             rwkv_r_k=rwkv_r_k, rwkv_gn_w=rwkv_gn_w, rwkv_gn_b=rwkv_gn_b, gla_gate_up=gla_gate_up,
             gla_gate_b=gla_gate_b, gla_norm_w=gla_norm_w, w_out_ab=w_out_ab, ln_ab_g=ln_ab_g,
             ln_ab_b=ln_ab_b, ffn_w1=ffn_w1, ffn_w3=ffn_w3, ffn_w2=ffn_w2, ln_ffn_g=ln_ffn_g,
             ln_ffn_b=ln_ffn_b, w_in_c=w_in_c, hgrn_norm_w=hgrn_norm_w, w_out_c=w_out_c, ln_c_g=ln_c_g,
             ln_c_b=ln_c_b, moe_router=moe_router, moe_w1=moe_w1, moe_w3=moe_w3, moe_w2=moe_w2,
             ln_moe_g=ln_moe_g, ln_moe_b=ln_moe_b)
    sm = jax.nn.softmax(hgrn_lb.astype(F32), axis=0)
    lower_bounds = jnp.cumsum(sm, axis=0) - sm[0]
    pb, pt, _ = x_prompt.shape
    sb, st, _ = x_sample.shape
    groups = [(pb, pt), (sb, st)]
    x = jnp.concatenate([_to_rows(x_prompt), _to_rows(x_sample)], axis=0)

    def with_prompt(state):
        return jnp.concatenate([jnp.zeros((state.shape[0], pb) + state.shape[2:], state.dtype), state], axis=1)

    y, s_rwkv, s_shift, s_gla, s_hgrn = _trunk(
        x, groups, with_prompt(state_rwkv), with_prompt(cache_rwkv_shift), with_prompt(state_gla),
        with_prompt(state_hgrn), lower_bounds, p)
    y_prompt = _from_rows(y[:pb * pt], pb, pt)
    y_sample = _from_rows(y[pb * pt:], sb, st)
    return (y_prompt, y_sample,
            s_rwkv[:, :pb], s_shift[:, :pb], s_gla[:, :pb], s_hgrn[:, :pb],
            s_rwkv[:, pb:], s_shift[:, pb:], s_gla[:, pb:], s_hgrn[:, pb:])
```

```python
import functools

import jax
import jax.numpy as jnp
import numpy as np
from jax import lax
from jax.experimental import pallas as pl
from jax.experimental.pallas import tpu as pltpu

F32 = jnp.float32
BF16 = jnp.bfloat16

D_MODEL = 1024
DEPTH = 4
CHUNK = 64

RWKV_HEADS = 8
RWKV_HD = 64
RWKV_DIM = RWKV_HEADS * RWKV_HD
DECAY_LORA = 64
AAA_LORA = 64
GATE_LORA = 128
RWKV_PROJ = 3 * RWKV_DIM + DECAY_LORA + AAA_LORA + GATE_LORA
RWKV_GN_EPS = RWKV_HD * 1e-5
DECAY_SCALE = 0.6065306597126334

GLA_HEADS = 4
GLA_DK = 64
GLA_DV = 128
GLA_KDIM = GLA_HEADS * GLA_DK
GLA_VDIM = GLA_HEADS * GLA_DV
GLA_RANK = 16
GLA_TAU = 16.0
GLA_GZW = 128

HGRN_HEADS = 8
HGRN_HD = 128
HGRN_DIM = HGRN_HEADS * HGRN_HD

D_FF = 2816
FF_CHUNK = 256
N_EXPERTS = 8
MOE_BLK = 512

ALPHA = (2 * DEPTH) ** 0.25
NORM_EPS = 1e-5

VMEM_LIMIT = 56 * 1024 * 1024
LANES = 128


def _cparams(*sem):
    return pltpu.CompilerParams(dimension_semantics=tuple(sem), vmem_limit_bytes=VMEM_LIMIT)


def _tile(m, pref):
    t = pref
    while m % t:
        t //= 2
    return t


def _sigmoid(x):
    return 1.0 / (1.0 + jnp.exp(-x))


def _layer_norm(y, g, b):
    mu = jnp.mean(y, axis=-1, keepdims=True)
    d = y - mu
    var = jnp.mean(d * d, axis=-1, keepdims=True)
    return d * lax.rsqrt(var + NORM_EPS) * g + b


def _dot(a, b):
    return jnp.dot(a.astype(BF16), b.astype(BF16), preferred_element_type=F32)


def _dot_nt(a, b):
    return lax.dot_general(a.astype(BF16), b.astype(BF16), (((1,), (1,)), ((), ())),
                           preferred_element_type=F32)


def _dot_tn(a, b):
    return lax.dot_general(a.astype(BF16), b.astype(BF16), (((0,), (0,)), ((), ())),
                           preferred_element_type=F32)


def _mm_kernel(x_ref, w_ref, o_ref):
    o_ref[...] = _dot(x_ref[...], w_ref[...]).astype(o_ref.dtype)


def _matmul(x, w, tm, tn, out_dtype=F32):
    m, k = x.shape
    n = w.shape[1]
    return pl.pallas_call(
        _mm_kernel,
        grid=(m // tm, n // tn),
        in_specs=[pl.BlockSpec((tm, k), lambda i, j: (i, 0)),
                  pl.BlockSpec((k, tn), lambda i, j: (0, j))],
        out_specs=pl.BlockSpec((tm, tn), lambda i, j: (i, j)),
        out_shape=jax.ShapeDtypeStruct((m, n), out_dtype),
        compiler_params=_cparams("parallel", "arbitrary"),
        name="matmul",
    )(x, w)


def _proj_ln_kernel(n_in, *refs):
    a_refs = refs[:n_in]
    w_refs = refs[n_in:2 * n_in]
    res_ref, g_ref, b_ref, o_ref = refs[2 * n_in:]
    acc = _dot(a_refs[0][...], w_refs[0][...])
    for a_ref, w_ref in zip(a_refs[1:], w_refs[1:]):
        acc = acc + _dot(a_ref[...], w_ref[...])
    o_ref[...] = _layer_norm(ALPHA * res_ref[...] + acc, g_ref[...], b_ref[...])


def _proj_ln(acts, ws, res, g, b, tm):
    m, d = res.shape
    n_in = len(acts)
    in_specs = [pl.BlockSpec((tm, a.shape[1]), lambda i: (i, 0)) for a in acts]
    in_specs += [pl.BlockSpec(w.shape, lambda i: (0, 0)) for w in ws]
    in_specs += [pl.BlockSpec((tm, d), lambda i: (i, 0)),
                 pl.BlockSpec((1, d), lambda i: (0, 0)),
                 pl.BlockSpec((1, d), lambda i: (0, 0))]
    return pl.pallas_call(
        functools.partial(_proj_ln_kernel, n_in),
        grid=(m // tm,),
        in_specs=in_specs,
        out_specs=pl.BlockSpec((tm, d), lambda i: (i, 0)),
        out_shape=jax.ShapeDtypeStruct((m, d), F32),
        compiler_params=_cparams("parallel"),
        name="proj_ln",
    )(*acts, *ws, res, g.reshape(1, d), b.reshape(1, d))


def _swiglu_acc(xb, w1_ref, w3_ref, w2_ref, acc_ref):
    acc_ref[...] = jnp.zeros_like(acc_ref)

    def chunk(c, carry):
        c0 = pl.multiple_of(c * FF_CHUNK, FF_CHUNK)
        h1 = jnp.dot(xb, w1_ref[:, pl.ds(c0, FF_CHUNK)].astype(BF16), preferred_element_type=F32)
        h3 = jnp.dot(xb, w3_ref[:, pl.ds(c0, FF_CHUNK)].astype(BF16), preferred_element_type=F32)
        hh = (h1 * _sigmoid(h1) * h3).astype(BF16)
        acc_ref[...] += jnp.dot(hh, w2_ref[pl.ds(c0, FF_CHUNK), :].astype(BF16), preferred_element_type=F32)
        return carry

    lax.fori_loop(0, D_FF // FF_CHUNK, chunk, 0)


def _ffn_ln_kernel(x_ref, w1_ref, w3_ref, w2_ref, g_ref, b_ref, o_ref, acc_ref):
    x = x_ref[...]
    _swiglu_acc(x.astype(BF16), w1_ref, w3_ref, w2_ref, acc_ref)
    o_ref[...] = _layer_norm(ALPHA * x + acc_ref[...], g_ref[...], b_ref[...])


def _ffn_ln(x, w1, w3, w2, g, b, tm):
    m, d = x.shape
    const = lambda i: (0, 0)
    return pl.pallas_call(
        _ffn_ln_kernel,
        grid=(m // tm,),
        in_specs=[pl.BlockSpec((tm, d), lambda i: (i, 0)),
                  pl.BlockSpec(w1.shape, const, pipeline_mode=pl.Buffered(1)),
                  pl.BlockSpec(w3.shape, const, pipeline_mode=pl.Buffered(1)),
                  pl.BlockSpec(w2.shape, const, pipeline_mode=pl.Buffered(1)),
                  pl.BlockSpec((1, d), const),
                  pl.BlockSpec((1, d), const)],
        out_specs=pl.BlockSpec((tm, d), lambda i: (i, 0)),
        out_shape=jax.ShapeDtypeStruct((m, d), F32),
        scratch_shapes=[pltpu.VMEM((tm, d), F32)],
        compiler_params=_cparams("parallel"),
        name="ffn_ln",
    )(x, w1, w3, w2, g.reshape(1, d), b.reshape(1, d))


def _to_rows(x):
    bsz, t_len, d = x.shape
    nb = t_len // CHUNK
    x = x.reshape(bsz // 2, 2, nb, CHUNK, d).transpose(0, 2, 1, 3, 4)
    return x.reshape(bsz * t_len, d)


def _from_rows(y, bsz, t_len):
    d = y.shape[-1]
    nb = t_len // CHUNK
    y = y.reshape(bsz // 2, nb, 2, CHUNK, d).transpose(0, 2, 1, 3, 4)
    return y.reshape(bsz, t_len, d)


def _seq_items(groups, per):
    blk, sidx, flags = [], [], []
    row0, s0 = 0, 0
    for bsz, t_len in groups:
        assert t_len % CHUNK == 0 and bsz % 2 == 0 and row0 % (2 * CHUNK) == 0
        nb = t_len // CHUNK
        for p in range(bsz // 2):
            for q in range(2 // per):
                for t in range(nb):
                    if per == 2:
                        blk.append(row0 // (2 * CHUNK) + p * nb + t)
                        sidx.append(s0 // 2 + p)
                    else:
                        blk.append(row0 // CHUNK + (p * nb + t) * 2 + q)
                        sidx.append(s0 + 2 * p + q)
                    flags.append(int(t == 0) + 2 * int(t == nb - 1))
        row0 += bsz * t_len
        s0 += bsz
    return tuple(jnp.asarray(np.asarray(a, np.int32)) for a in (blk, sidx, flags))


RW_G = 2 * RWKV_HEADS
RW_PAIRS = RW_G // 2
RW_UNROLL = 8


def _halves(x, lo):
    sa = jnp.sum(jnp.where(lo, x, 0.0), axis=1, keepdims=True)
    sb = jnp.sum(jnp.where(lo, 0.0, x), axis=1, keepdims=True)
    return jnp.where(lo, sa, sb)


def _rwkv_kernel(blk_ref, sidx_ref, flag_ref,
                 pr_ref, sh_ref, s0_ref,
                 mu_ref, w0_ref, wup_ref, a0_ref, aup_ref, gup_ref,
                 kk_ref, ka_ref, rk_ref, gnw_ref, gnb_ref, onehot_ref,
                 o_ref, st_ref, sho_ref, *scr):
    st = scr[:RW_PAIRS]
    prev_scr, ops_scr, post_scr, vt_scr, vb_scr, oblk_scr = scr[RW_PAIRS:]
    i = pl.program_id(0)
    flag = flag_ref[i]
    tt = CHUNK
    rows = 2 * tt
    hd = RWKV_HD

    @pl.when((flag & 1) == 1)
    def _():
        for q in range(2):
            for p in range(RWKV_HEADS // 2):
                st[q * 4 + p][...] = jnp.concatenate([s0_ref[q, 2 * p], s0_ref[q, 2 * p + 1]], axis=1)
            prev_scr[q] = sh_ref[q]

    rowi = lax.broadcasted_iota(jnp.int32, (rows, 1), 0)
    lo_rows = lax.broadcasted_iota(jnp.int32, (rows, LANES), 1) < hd

    x = pr_ref[...]
    xprev = jnp.where(rowi == 0, prev_scr[0],
                      jnp.where(rowi == tt, prev_scr[1], pltpu.roll(x, 1, 0)))
    prev_scr[0] = x[tt - 1:tt, :]
    prev_scr[1] = x[rows - 1:rows, :]
    xs = x + (xprev - x) * mu_ref[...]
    r = xs[:, 0:RWKV_DIM]
    k = xs[:, RWKV_DIM:2 * RWKV_DIM]
    v = xs[:, 2 * RWKV_DIM:3 * RWKV_DIM]
    lora = xs[:, 3 * RWKV_DIM:3 * RWKV_DIM + DECAY_LORA + AAA_LORA]
    gd = xs[:, 3 * RWKV_DIM + DECAY_LORA + AAA_LORA:]
    w = jnp.exp(-DECAY_SCALE * _sigmoid(w0_ref[...] + _dot(jnp.tanh(lora), wup_ref[...])))
    a = _sigmoid(a0_ref[...] + _dot(lora, aup_ref[...]))
    g = _dot(_sigmoid(gd), gup_ref[...])
    kkraw = k * kk_ref[...]
    kt = k * (1.0 + (a - 1.0) * ka_ref[...])
    prod = r * kt * rk_ref[...]
    bsums = []
    zpad = jnp.zeros((tt, LANES), F32)
    for p in range(RWKV_HEADS // 2):
        cs = slice(LANES * p, LANES * (p + 1))
        kk_p = kkraw[:, cs]
        kk_p = kk_p / jnp.maximum(jnp.sqrt(_halves(kk_p * kk_p, lo_rows)), 1e-12)
        kka_p = kk_p * a[:, cs]
        bsums.append(_halves(prod[:, cs], lo_rows))
        for q in range(2):
            pp = q * 4 + p
            rs = slice(q * tt, (q + 1) * tt)
            ops_scr[0, pp] = r[rs, cs]
            ops_scr[1, pp] = w[rs, cs]
            ops_scr[2, pp] = kt[rs, cs]
            ops_scr[3, pp] = kk_p[rs].astype(BF16).astype(F32)
            ops_scr[4, pp] = kka_p[rs]
            vt = jnp.concatenate([v[rs, cs], zpad], axis=0).T[:, :tt]
            hi = vt.astype(BF16).astype(F32)
            low = vt - hi
            vt_scr[pp * hd:(pp + 1) * hd, :] = jnp.concatenate(
                [hi[:hd], low[:hd], hi[hd:], low[hd:]], axis=1).astype(BF16)
    post_scr[0] = g
    post_scr[1] = jnp.concatenate(bsums, axis=1) * v

    lo = lax.broadcasted_iota(jnp.int32, (hd, LANES), 1) < hd
    lo8 = lax.broadcasted_iota(jnp.int32, (8, LANES), 1) < hd
    row8 = lax.broadcasted_iota(jnp.int32, (8, LANES), 0)
    for c in range(tt // 8):
        cs = slice(8 * LANES * c, 8 * LANES * (c + 1))
        vb_scr[:, cs] = jnp.dot(vt_scr[...], onehot_ref[:, cs], preferred_element_type=F32)

    def step_group(tg, carry):
        t0 = tg * RW_UNROLL
        states = [st[pp][...] for pp in range(RW_PAIRS)]
        rounded = [s.astype(BF16) for s in states]
        for j in range(RW_UNROLL):
            t = t0 + j
            for pp in range(RW_PAIRS):
                s = states[pp]
                r_t = ops_scr[0, pp, pl.ds(t, 1), :]
                w_t = ops_scr[1, pp, pl.ds(t, 1), :]
                k_t = ops_scr[2, pp, pl.ds(t, 1), :]
                kk_t = ops_scr[3, pp, pl.ds(t, 1), :]
                kka_t = ops_scr[4, pp, pl.ds(t, 1), :]
                sa = _halves(rounded[pp].astype(F32) * kk_t, lo)
                v_t = vb_scr[pp * hd:(pp + 1) * hd, pl.ds(pl.multiple_of(t * LANES, LANES), LANES)]
                s = s * w_t - sa * kka_t + v_t * k_t
                states[pp] = s
                rounded[pp] = s.astype(BF16)
                r2 = jnp.where(row8 == 0, jnp.where(lo8, r_t, 0.0),
                               jnp.where(row8 == 1, jnp.where(lo8, 0.0, r_t), 0.0))
                o2 = _dot_nt(r2, rounded[pp])
                oblk_scr[2 * pp, pl.ds(t, 1), :] = o2[0:1]
                oblk_scr[2 * pp + 1, pl.ds(t, 1), :] = o2[1:2]
        for pp in range(RW_PAIRS):
            st[pp][...] = states[pp]
        return carry

    lax.fori_loop(0, tt // RW_UNROLL, step_group, 0)

    def group_norm(o):
        mu = jnp.mean(o, axis=1, keepdims=True)
        d = o - mu
        var = jnp.mean(d * d, axis=1, keepdims=True)
        return d * lax.rsqrt(var + RWKV_GN_EPS)

    on = jnp.concatenate(
        [jnp.concatenate([group_norm(oblk_scr[q * RWKV_HEADS + h]) for h in range(RWKV_HEADS)], axis=1)
         for q in range(2)], axis=0)
    out = (on * gnw_ref[...] + gnb_ref[...] + post_scr[1]) * post_scr[0]
    o_ref[...] = out.astype(o_ref.dtype)

    @pl.when((flag & 2) == 2)
    def _():
        for q in range(2):
            for p in range(RWKV_HEADS // 2):
                s = st[q * 4 + p][...]
                st_ref[q, 2 * p] = s[:, :hd]
                st_ref[q, 2 * p + 1] = s[:, hd:]
            sho_ref[q] = prev_scr[q]


def _value_onehot():
    r = np.arange(4 * CHUNK)[:, None]
    c = np.arange(CHUNK * LANES)[None, :]
    hit = ((r % CHUNK) == (c // LANES)) & ((r // (2 * CHUNK)) == ((c % LANES) // RWKV_HD))
    return jnp.asarray(hit.astype(np.float32), BF16)


def _rwkv_mix(proj, groups, shift0, s0, params):
    rows = proj.shape[0]
    n_seq = s0.shape[0]
    blk, sidx, flags = _seq_items(groups, 2)
    full = lambda a: pl.BlockSpec(a.shape, lambda i, b, s, f: (0,) * a.ndim)
    st_spec = pl.BlockSpec((2, RWKV_HEADS, RWKV_HD, RWKV_HD), lambda i, b, s, f: (s[i], 0, 0, 0))
    sh_spec = pl.BlockSpec((2, 1, RWKV_PROJ), lambda i, b, s, f: (s[i], 0, 0))
    grid_spec = pltpu.PrefetchScalarGridSpec(
        num_scalar_prefetch=3,
        grid=(flags.shape[0],),
        in_specs=[pl.BlockSpec((2 * CHUNK, RWKV_PROJ), lambda i, b, s, f: (b[i], 0)),
                  sh_spec, st_spec] + [full(a) for a in params]
                 + [pl.BlockSpec((4 * CHUNK, CHUNK * LANES), lambda i, b, s, f: (0, 0),
                                 pipeline_mode=pl.Buffered(1))],
        out_specs=[pl.BlockSpec((2 * CHUNK, RWKV_DIM), lambda i, b, s, f: (b[i], 0)), st_spec, sh_spec],
        scratch_shapes=[pltpu.VMEM((RWKV_HD, LANES), F32) for _ in range(RW_PAIRS)]
                       + [pltpu.VMEM((2, 1, RWKV_PROJ), F32),
                          pltpu.VMEM((5, RW_PAIRS, CHUNK, LANES), F32),
                          pltpu.VMEM((2, 2 * CHUNK, RWKV_DIM), F32),
                          pltpu.VMEM((RW_PAIRS * RWKV_HD, 4 * CHUNK), BF16),
                          pltpu.VMEM((RW_PAIRS * RWKV_HD, CHUNK * LANES), F32),
                          pltpu.VMEM((RW_G, CHUNK, RWKV_HD), F32)],
    )
    return pl.pallas_call(
        _rwkv_kernel,
        grid_spec=grid_spec,
        out_shape=[jax.ShapeDtypeStruct((rows, RWKV_DIM), BF16),
                   jax.ShapeDtypeStruct((n_seq, RWKV_HEADS, RWKV_HD, RWKV_HD), F32),
                   jax.ShapeDtypeStruct((n_seq, 1, RWKV_PROJ), F32)],
        compiler_params=_cparams("arbitrary"),
        name="rwkv_mix",
    )(blk, sidx, flags, proj, shift0, s0, *params, _value_onehot())


SUB = 16
LEVELS = (16, 8, 4, 2)
G_PIECES = 2
EX_B, EX_TAIL, EX_Q, EX_K1 = 0, 1, 2, 3
EX_LVL = EX_K1 + CHUNK // SUB - 1
EX_BLOCKS = EX_LVL + len(LEVELS)


def _chunk_tables():
    r = np.arange(CHUNK)[:, None]
    c = np.arange(CHUNK)[None, :]
    mats = [c <= r, c > r, (c >= (r // SUB) * SUB) & (c <= r)]
    for i in range(1, CHUNK // SUB):
        mats.append((c > r) & (c <= SUB * i - 1))
    masks = []
    for size in LEVELS:
        half = size // 2
        mid = (r // size) * size + half - 1
        upper = (r % size) >= half
        mats.append(np.where(upper, (c > mid) & (c <= r), (c > r) & (c <= mid)))
        masks.append(((r // size) == (c // size)) & upper & ((c % size) < half))
    e = np.concatenate(mats, axis=0).astype(np.float32)
    return (jnp.asarray(np.concatenate([e] * G_PIECES, axis=1), BF16),
            jnp.asarray(np.stack(masks).astype(np.float32)))


def _chunk_consts(dk):
    row = lax.broadcasted_iota(jnp.int32, (CHUNK, 1), 0)
    r = lax.broadcasted_iota(jnp.int32, (CHUNK, CHUNK), 0)
    c = lax.broadcasted_iota(jnp.int32, (CHUNK, CHUNK), 1)
    return dict(
        upper=[(row % size) >= size // 2 for size in LEVELS],
        blkrow=row // SUB,
        cross=(r // SUB) > (c // SUB),
        eye_t=r == c,
        eye=(lax.broadcasted_iota(jnp.int32, (dk, dk), 0)
             == lax.broadcasted_iota(jnp.int32, (dk, dk), 1)),
    )


def _split_bf16(x, pieces):
    out = []
    for _ in range(pieces - 1):
        p = x.astype(BF16)
        out.append(p)
        x = x - p.astype(F32)
    out.append(x.astype(BF16))
    return out


def _chunk_core(qs, ks, vs, gs, ss, ecat_ref, lmask_ref, cst):
    n = len(qs)
    ecat = ecat_ref[...]
    exs = [jnp.exp(jnp.dot(ecat, jnp.concatenate(_split_bf16(g, G_PIECES), axis=0),
                           preferred_element_type=F32)) for g in gs]
    blk = lambda ex, i: ex[i * CHUNK:(i + 1) * CHUNK]
    o_inter = [_dot(qs[h] * blk(exs[h], EX_B), ss[h]) for h in range(n)]
    p_cross = []
    for h in range(n):
        qg = qs[h] * blk(exs[h], EX_Q)
        rows = [jnp.zeros((SUB, CHUNK), F32)]
        for i in range(1, CHUNK // SUB):
            rows.append(_dot_nt(qg[SUB * i:SUB * (i + 1)], ks[h] * blk(exs[h], EX_K1 + i - 1)))
        p_cross.append(jnp.concatenate(rows, axis=0))
    p_diag = []
    for h in range(n):
        acc = jnp.zeros((CHUNK, CHUNK), F32)
        for lvl in range(len(LEVELS)):
            a = jnp.where(cst["upper"][lvl], qs[h], ks[h]) * blk(exs[h], EX_LVL + lvl)
            a_hi, a_lo = _split_bf16(a, 2)
            acc = acc + lmask_ref[lvl] * lax.dot_general(
                jnp.concatenate([a_hi, a_hi, a_lo], axis=1), jnp.concatenate([a_hi, a_lo, a_hi], axis=1),
                (((1,), (1,)), ((), ())), preferred_element_type=F32)
        p_diag.append(jnp.where(cst["eye_t"], jnp.sum(qs[h] * ks[h], axis=1, keepdims=True), acc))
    os_ = [o_inter[h] + _dot(jnp.where(cst["cross"], p_cross[h], p_diag[h]), vs[h]) for h in range(n)]
    s_new = []
    for h in range(n):
        dcol = jnp.sum(jnp.where(cst["eye"], exs[h][CHUNK - 1:CHUNK], 0.0), axis=1, keepdims=True)
        s_new.append(ss[h] * dcol + _dot_tn(ks[h] * blk(exs[h], EX_TAIL), vs[h]))
    return os_, s_new


def _rms_norm(o, w):
    return o * lax.rsqrt(jnp.mean(o * o, axis=-1, keepdims=True) + NORM_EPS) * w


def _log_sigmoid(x):
    return jnp.minimum(x, 0.0) - jnp.log1p(jnp.exp(-jnp.abs(x)))


GLA_DKP = 128
GLA_MAIN = 4 * GLA_HEADS * 128


def _gla_kernel(blk_ref, sidx_ref, flag_ref,
                gl_ref, gz_ref, s0_ref, gup_ref, gb_ref, nw_ref, ecat_ref, lmask_ref,
                o_ref, st_ref, *st):
    i = pl.program_id(0)
    flag = flag_ref[i]
    hw = GLA_HEADS * GLA_DKP

    @pl.when((flag & 1) == 1)
    def _():
        for q in range(2):
            for h in range(GLA_HEADS):
                st[q * GLA_HEADS + h][...] = s0_ref[q, h]

    cst = _chunk_consts(GLA_DKP)
    log_a = _log_sigmoid(_dot(gz_ref[...], gup_ref[...]) + gb_ref[...]) / GLA_TAU
    qs, ks, vs, gs = [], [], [], []
    for q in range(2):
        rs = slice(q * CHUNK, (q + 1) * CHUNK)
        for h in range(GLA_HEADS):
            qs.append(gl_ref[rs, GLA_DKP * h:GLA_DKP * (h + 1)] * GLA_DK ** -0.5)
            ks.append(gl_ref[rs, hw + GLA_DKP * h:hw + GLA_DKP * (h + 1)])
            vs.append(gl_ref[rs, 2 * hw + GLA_DV * h:2 * hw + GLA_DV * (h + 1)])
            gs.append(log_a[rs, GLA_DKP * h:GLA_DKP * (h + 1)])
    os_, s_new = _chunk_core(qs, ks, vs, gs, [r[...] for r in st], ecat_ref, lmask_ref, cst)
    for r, s in zip(st, s_new):
        r[...] = s
    normed = [_rms_norm(o, nw_ref[...]) for o in os_]
    on = jnp.concatenate([jnp.concatenate(normed[q * GLA_HEADS:(q + 1) * GLA_HEADS], axis=1)
                          for q in range(2)], axis=0)
    og = gl_ref[:, 3 * hw:3 * hw + GLA_VDIM]
    o_ref[...] = (on * (og * _sigmoid(og))).astype(o_ref.dtype)

    @pl.when((flag & 2) == 2)
    def _():
        for q in range(2):
            for h in range(GLA_HEADS):
                st_ref[q, h] = st[q * GLA_HEADS + h][...]


def _gla_mix(proj, groups, s0, gup, gb, nw, tables):
    rows = proj.shape[0]
    blk, sidx, flags = _seq_items(groups, 2)
    full = lambda a: pl.BlockSpec(a.shape, lambda i, b, s, f: (0,) * a.ndim)
    st_spec = pl.BlockSpec((2, GLA_HEADS, GLA_DKP, GLA_DV), lambda i, b, s, f: (s[i], 0, 0, 0))
    grid_spec = pltpu.PrefetchScalarGridSpec(
        num_scalar_prefetch=3,
        grid=(flags.shape[0],),
        in_specs=[pl.BlockSpec((2 * CHUNK, GLA_MAIN), lambda i, b, s, f: (b[i], 1)),
                  pl.BlockSpec((2 * CHUNK, GLA_GZW), lambda i, b, s, f: (b[i], RWKV_PROJ // GLA_GZW)),
                  st_spec, full(gup), full(gb), full(nw), full(tables[0]), full(tables[1])],
        out_specs=[pl.BlockSpec((2 * CHUNK, GLA_VDIM), lambda i, b, s, f: (b[i], 0)), st_spec],
        scratch_shapes=[pltpu.VMEM((GLA_DKP, GLA_DV), F32) for _ in range(2 * GLA_HEADS)],
    )
    return pl.pallas_call(
        _gla_kernel,
        grid_spec=grid_spec,
        out_shape=[jax.ShapeDtypeStruct((rows, GLA_VDIM), BF16),
                   jax.ShapeDtypeStruct(s0.shape, F32)],
        compiler_params=_cparams("arbitrary"),
        name="gla_mix",
    )(blk, sidx, flags, proj, proj, s0, gup, gb, nw, *tables)


def _hgrn_kernel(blk_ref, sidx_ref, flag_ref,
                 pc_ref, s0_ref, lb_ref, nw_ref, ecat_ref, lmask_ref,
                 o_ref, st_ref, *st):
    i = pl.program_id(0)
    flag = flag_ref[i]

    @pl.when((flag & 1) == 1)
    def _():
        for h in range(HGRN_HEADS):
            st[h][...] = s0_ref[0, h]

    cst = _chunk_consts(HGRN_HD)
    qs, ks, vs, gs = [], [], [], []
    for h in range(HGRN_HEADS):
        hs = slice(HGRN_HD * h, HGRN_HD * (h + 1))
        qx = pc_ref[:, HGRN_HD * h:HGRN_HD * (h + 1)]
        fx = pc_ref[:, HGRN_DIM + HGRN_HD * h:HGRN_DIM + HGRN_HD * (h + 1)]
        lb = lb_ref[:, hs]
        qs.append(qx * _sigmoid(qx))
        gs.append(jnp.log(lb + (1.0 - lb) * _sigmoid(fx)))
        ks.append((1.0 - lb) * _sigmoid(-fx))
        vs.append(pc_ref[:, 2 * HGRN_DIM + HGRN_HD * h:2 * HGRN_DIM + HGRN_HD * (h + 1)])
    os_, s_new = _chunk_core(qs, ks, vs, gs, [r[...] for r in st], ecat_ref, lmask_ref, cst)
    for r, s in zip(st, s_new):
        r[...] = s
    on = jnp.concatenate([_rms_norm(o, nw_ref[...]) for o in os_], axis=1)
    gx = pc_ref[:, 3 * HGRN_DIM:4 * HGRN_DIM]
    o_ref[...] = (on * (gx * _sigmoid(gx))).astype(o_ref.dtype)

    @pl.when((flag & 2) == 2)
    def _():
        for h in range(HGRN_HEADS):
            st_ref[0, h] = st[h][...]


def _hgrn_mix(proj, groups, s0, lb, nw, tables):
    rows = proj.shape[0]
    blk, sidx, flags = _seq_items(groups, 1)
    full = lambda a: pl.BlockSpec(a.shape, lambda i, b, s, f: (0,) * a.ndim)
    st_spec = pl.BlockSpec((1, HGRN_HEADS, HGRN_HD, HGRN_HD), lambda i, b, s, f: (s[i], 0, 0, 0))
    grid_spec = pltpu.PrefetchScalarGridSpec(
        num_scalar_prefetch=3,
        grid=(flags.shape[0],),
        in_specs=[pl.BlockSpec((CHUNK, 4 * HGRN_DIM), lambda i, b, s, f: (b[i], 0)),
                  st_spec, full(lb), full(nw), full(tables[0]), full(tables[1])],
        out_specs=[pl.BlockSpec((CHUNK, HGRN_DIM), lambda i, b, s, f: (b[i], 0)), st_spec],
        scratch_shapes=[pltpu.VMEM((HGRN_HD, HGRN_HD), F32) for _ in range(HGRN_HEADS)],
    )
    return pl.pallas_call(
        _hgrn_kernel,
        grid_spec=grid_spec,
        out_shape=[jax.ShapeDtypeStruct((rows, HGRN_DIM), BF16),
                   jax.ShapeDtypeStruct(s0.shape, F32)],
        compiler_params=_cparams("arbitrary"),
        name="hgrn_mix",
    )(blk, sidx, flags, proj, s0, lb, nw, *tables)


ROUTER_TM = 1024
MOE_TQ = 512
DMA_UNROLL = 8
META_E1, META_E2, META_P1, META_P2, META_R1, META_R2 = range(6)


def _router_kernel(x_ref, wr_ref, meta_ref, cnt_ref, carry_scr):
    i = pl.program_id(0)

    @pl.when(i == 0)
    def _():
        carry_scr[...] = jnp.zeros_like(carry_scr)

    tm = x_ref.shape[0]
    logits = _dot(x_ref[...], wr_ref[...])
    lane = lax.broadcasted_iota(jnp.int32, (tm, LANES), 1)
    valid = lane < N_EXPERTS
    lg = jnp.where(valid, logits, -1e30)
    ex = jnp.where(valid, jnp.exp(lg - jnp.max(lg, axis=-1, keepdims=True)), 0.0)
    probs = ex / jnp.sum(ex, axis=-1, keepdims=True)
    p1 = jnp.max(probs, axis=-1, keepdims=True)
    i1 = jnp.min(jnp.where(jnp.logical_and(probs == p1, valid), lane, LANES), axis=-1, keepdims=True)
    rest = jnp.where(jnp.logical_or(lane == i1, jnp.logical_not(valid)), -1.0, probs)
    p2 = jnp.max(rest, axis=-1, keepdims=True)
    i2 = jnp.min(jnp.where(rest == p2, lane, LANES), axis=-1, keepdims=True)
    den = p1 + p2
    oh1 = (lane == i1).astype(F32)
    oh2 = (lane == i2).astype(F32)
    both = oh1 + oh2
    rr = lax.broadcasted_iota(jnp.int32, (tm, tm), 0)
    cc = lax.broadcasted_iota(jnp.int32, (tm, tm), 1)
    before = _dot((rr > cc).astype(F32), both) + carry_scr[...]
    r1 = jnp.sum(oh1 * before, axis=-1, keepdims=True)
    r2 = jnp.sum(oh2 * before, axis=-1, keepdims=True)
    carry_scr[...] += jnp.sum(both, axis=0, keepdims=True)
    meta = jnp.zeros((tm, LANES), F32)
    for idx, col in ((META_E1, i1.astype(F32)), (META_E2, i2.astype(F32)),
                     (META_P1, p1 / den), (META_P2, p2 / den), (META_R1, r1), (META_R2, r2)):
        meta = jnp.where(lane == idx, col, meta)
    meta_ref[...] = meta
    cnt_ref[...] = jnp.broadcast_to(carry_scr[...], cnt_ref.shape)


def _router(x, wr, tm):
    m, d = x.shape
    return pl.pallas_call(
        _router_kernel,
        grid=(m // tm,),
        in_specs=[pl.BlockSpec((tm, d), lambda i: (i, 0)),
                  pl.BlockSpec(wr.shape, lambda i: (0, 0))],
        out_specs=[pl.BlockSpec((tm, LANES), lambda i: (i, 0)),
                   pl.BlockSpec((8, LANES), lambda i: (0, 0))],
        out_shape=[jax.ShapeDtypeStruct((m, LANES), F32),
                   jax.ShapeDtypeStruct((8, LANES), F32)],
        scratch_shapes=[pltpu.VMEM((1, LANES), F32)],
        compiler_params=_cparams("arbitrary"),
        name="moe_router",
    )(x, wr)


def _scatter_kernel(dest_ref, x_ref, xs_in_ref, xs_ref, idx_smem, sem_idx, sem):
    del xs_in_ref
    i = pl.program_id(0)
    tq = x_ref.shape[0]
    cp = pltpu.make_async_copy(dest_ref.at[i], idx_smem, sem_idx)
    cp.start()
    cp.wait()

    def row_copy(t, slot):
        return pltpu.make_async_copy(x_ref.at[pl.ds(t, 1)],
                                     xs_ref.at[pl.ds(idx_smem[slot * tq + t], 1)], sem)

    def issue(t, carry):
        row_copy(t, 0).start()
        row_copy(t, 1).start()
        return carry

    def drain(t, carry):
        row_copy(t, 0).wait()
        row_copy(t, 1).wait()
        return carry

    lax.fori_loop(0, tq, issue, 0, unroll=DMA_UNROLL)
    lax.fori_loop(0, tq, drain, 0, unroll=DMA_UNROLL)


def _moe_scatter(x, dest, n_rows, tq):
    m, d = x.shape
    zeros = jnp.zeros((n_rows, d), x.dtype)
    any_spec = pl.BlockSpec(memory_space=pl.ANY)
    return pl.pallas_call(
        _scatter_kernel,
        grid=(m // tq,),
        in_specs=[any_spec, pl.BlockSpec((tq, d), lambda i: (i, 0)), any_spec],
        out_specs=any_spec,
        out_shape=jax.ShapeDtypeStruct((n_rows, d), x.dtype),
        scratch_shapes=[pltpu.SMEM((2 * tq,), jnp.int32),
                        pltpu.SemaphoreType.DMA(()),
                        pltpu.SemaphoreType.DMA(())],
        input_output_aliases={2: 0},
        compiler_params=pltpu.CompilerParams(dimension_semantics=("arbitrary",),
                                             has_side_effects=True),
        name="moe_scatter",
    )(dest, x, zeros)


def _expert_kernel(be_ref, bv_ref, x_ref, w1_ref, w3_ref, w2_ref, y_ref, acc_ref):
    b = pl.program_id(0)

    @pl.when(bv_ref[b] == 1)
    def _():
        _swiglu_acc(x_ref[...].astype(BF16), w1_ref, w3_ref, w2_ref, acc_ref)
        y_ref[...] = acc_ref[...]

    @pl.when(bv_ref[b] == 0)
    def _():
        y_ref[...] = jnp.zeros_like(y_ref)


def _moe_experts(xs, block_expert, block_valid, j, w1, w3, w2):
    rows, d = xs.shape
    wspec = lambda w: pl.BlockSpec((None, None) + w.shape[2:], lambda b, be, bv: (j, be[b], 0, 0),
                                   pipeline_mode=pl.Buffered(1))
    grid_spec = pltpu.PrefetchScalarGridSpec(
        num_scalar_prefetch=2,
        grid=(rows // MOE_BLK,),
        in_specs=[pl.BlockSpec((MOE_BLK, d), lambda b, be, bv: (b, 0)),
                  wspec(w1), wspec(w3), wspec(w2)],
        out_specs=pl.BlockSpec((MOE_BLK, d), lambda b, be, bv: (b, 0)),
        scratch_shapes=[pltpu.VMEM((MOE_BLK, d), F32)],
    )
    return pl.pallas_call(
        _expert_kernel,
        grid_spec=grid_spec,
        out_shape=jax.ShapeDtypeStruct((rows, d), F32),
        compiler_params=_cparams("arbitrary"),
        name="moe_experts",
    )(block_expert, block_valid, xs, w1, w3, w2)


def _combine_kernel(dest_ref, y_ref, x_ref, meta_ref, g_ref, b_ref, o_ref,
                    idx_smem, ya_scr, yb_scr, sem_idx, sem):
    i = pl.program_id(0)
    tq = x_ref.shape[0]
    par = i % 2

    def row_copy(t, slot, p):
        buf = ya_scr if slot == 0 else yb_scr
        return pltpu.make_async_copy(y_ref.at[pl.ds(idx_smem[p, slot * tq + t], 1)],
                                     buf.at[p, pl.ds(t, 1)], sem.at[p])

    def fetch(step, p):
        cp = pltpu.make_async_copy(dest_ref.at[step], idx_smem.at[p], sem_idx)
        cp.start()
        cp.wait()

        def issue(t, carry):
            row_copy(t, 0, p).start()
            row_copy(t, 1, p).start()
            return carry

        lax.fori_loop(0, tq, issue, 0, unroll=DMA_UNROLL)

    @pl.when(i == 0)
    def _():
        fetch(0, 0)

    @pl.when(i + 1 < pl.num_programs(0))
    def _():
        fetch(i + 1, 1 - par)

    def drain(t, carry):
        row_copy(t, 0, par).wait()
        row_copy(t, 1, par).wait()
        return carry

    lax.fori_loop(0, tq, drain, 0, unroll=DMA_UNROLL)
    meta = meta_ref[...]
    moe = ya_scr[par] * meta[:, META_P1:META_P1 + 1] + yb_scr[par] * meta[:, META_P2:META_P2 + 1]
    o_ref[...] = _layer_norm(ALPHA * x_ref[...] + moe, g_ref[...], b_ref[...])


def _moe_combine(dest, y, x, meta, g, b, tq):
    m, d = x.shape
    any_spec = pl.BlockSpec(memory_space=pl.ANY)
    const = lambda i: (0, 0)
    return pl.pallas_call(
        _combine_kernel,
        grid=(m // tq,),
        in_specs=[any_spec, any_spec,
                  pl.BlockSpec((tq, d), lambda i: (i, 0)),
                  pl.BlockSpec((tq, LANES), lambda i: (i, 0)),
                  pl.BlockSpec((1, d), const), pl.BlockSpec((1, d), const)],
        out_specs=pl.BlockSpec((tq, d), lambda i: (i, 0)),
        out_shape=jax.ShapeDtypeStruct((m, d), F32),
        scratch_shapes=[pltpu.SMEM((2, 2 * tq), jnp.int32),
                        pltpu.VMEM((2, tq, d), F32), pltpu.VMEM((2, tq, d), F32),
                        pltpu.SemaphoreType.DMA(()),
                        pltpu.SemaphoreType.DMA((2,))],
        compiler_params=_cparams("arbitrary"),
        name="moe_combine",
    )(dest, y, x, meta, g.reshape(1, d), b.reshape(1, d))


def _moe_ln(x, router, j, w1, w3, w2, g, b):
    m, d = x.shape
    tq = _tile(m, MOE_TQ)
    wr = jnp.pad(router, ((0, 0), (0, LANES - N_EXPERTS)))
    meta, cnt = _router(x, wr, _tile(m, ROUTER_TM))
    counts = cnt[0, :N_EXPERTS].astype(jnp.int32)
    padded = (counts + MOE_BLK - 1) // MOE_BLK * MOE_BLK
    pend = jnp.cumsum(padded)
    pstart = pend - padded
    n_blocks = (2 * m) // MOE_BLK + N_EXPERTS
    e1 = meta[:, META_E1].astype(jnp.int32)
    e2 = meta[:, META_E2].astype(jnp.int32)
    onehot = lambda e: (e[:, None] == jnp.arange(N_EXPERTS)[None, :]).astype(jnp.int32)
    dest1 = jnp.sum(onehot(e1) * pstart[None, :], axis=1) + meta[:, META_R1].astype(jnp.int32)
    dest2 = jnp.sum(onehot(e2) * pstart[None, :], axis=1) + meta[:, META_R2].astype(jnp.int32)
    dest = jnp.concatenate([dest1.reshape(m // tq, tq), dest2.reshape(m // tq, tq)], axis=1)
    block_start = jnp.arange(n_blocks, dtype=jnp.int32) * MOE_BLK
    block_expert = jnp.minimum(jnp.sum((block_start[:, None] >= pend[None, :]).astype(jnp.int32), axis=1),
                               N_EXPERTS - 1).astype(jnp.int32)
    block_valid = (block_start < pend[-1]).astype(jnp.int32)
    xs = _moe_scatter(x, dest, n_blocks * MOE_BLK, tq)
    y = _moe_experts(xs, block_expert, block_valid, j, w1, w3, w2)
    return _moe_combine(dest, y, x, meta, g, b, tq)


def _even_layer(x, j, groups, shift0, s_rwkv, s_gla, tables, p):
    m = x.shape[0]
    w_in = p["w_in_ab"][j]
    pg = w_in[:, RWKV_PROJ:]
    zcol = lambda n: jnp.zeros((D_MODEL, n), F32)

    def pad_heads(wq):
        wq = wq.reshape(D_MODEL, GLA_HEADS, GLA_DK)
        return jnp.pad(wq, ((0, 0), (0, 0), (0, GLA_DKP - GLA_DK))).reshape(D_MODEL, GLA_HEADS * GLA_DKP)

    gz0 = 2 * GLA_KDIM + GLA_VDIM
    packed = jnp.concatenate(
        [w_in[:, :RWKV_PROJ],
         pg[:, gz0:gz0 + GLA_RANK], zcol(2 * GLA_GZW - GLA_RANK),
         pad_heads(pg[:, 0:GLA_KDIM]), pad_heads(pg[:, GLA_KDIM:2 * GLA_KDIM]),
         pg[:, 2 * GLA_KDIM:gz0],
         pg[:, gz0 + GLA_RANK:]], axis=1).astype(BF16)
    proj = _matmul(x, packed, _tile(m, 1024), 2048)
    zl = jnp.zeros((DECAY_LORA, RWKV_DIM), F32)
    row = lambda a: a.reshape(1, -1)
    rw_params = [row(p["rwkv_mu"][j]), row(p["rwkv_w0"][j]),
                 jnp.concatenate([p["rwkv_w_up"][j], zl], axis=0).astype(BF16),
                 row(p["rwkv_a0"][j]),
                 jnp.concatenate([zl, p["rwkv_a_up"][j]], axis=0).astype(BF16),
                 p["rwkv_g_up"][j].astype(BF16),
                 row(p["rwkv_k_k"][j]), row(p["rwkv_k_a"][j]), row(p["rwkv_r_k"][j]),
                 row(p["rwkv_gn_w"][j]), row(p["rwkv_gn_b"][j])]
    o_rwkv, s_rwkv, shift_new = _rwkv_mix(proj, groups, shift0, s_rwkv, rw_params)
    pad_k = lambda a: jnp.pad(a.reshape(-1, GLA_HEADS, GLA_DK),
                              ((0, 0), (0, 0), (0, GLA_DKP - GLA_DK))).reshape(-1, GLA_HEADS * GLA_DKP)
    gup = jnp.pad(pad_k(p["gla_gate_up"][j]), ((0, GLA_GZW - GLA_RANK), (0, 0))).astype(BF16)
    s_gla_p = jnp.pad(s_gla, ((0, 0), (0, 0), (0, GLA_DKP - GLA_DK), (0, 0)))
    o_gla, s_gla_p = _gla_mix(proj, groups, s_gla_p, gup, pad_k(row(p["gla_gate_b"][j])),
                              row(p["gla_norm_w"][j]), tables)
    w_out = p["w_out_ab"][j].astype(BF16)
    x = _proj_ln([o_rwkv, o_gla], [w_out[:RWKV_DIM], w_out[RWKV_DIM:]], x,
                 p["ln_ab_g"][j], p["ln_ab_b"][j], _tile(m, 1024))
    x = _ffn_ln(x, p["ffn_w1"][j].astype(BF16), p["ffn_w3"][j].astype(BF16), p["ffn_w2"][j].astype(BF16),
                p["ln_ffn_g"][j], p["ln_ffn_b"][j], _tile(m, 1024))
    return x, shift_new[:, 0, :], s_rwkv, s_gla_p[:, :, :GLA_DK, :]


def _odd_layer(x, j, groups, lb, s_hgrn, tables, p):
    m = x.shape[0]
    proj = _matmul(x, p["w_in_c"][j].astype(BF16), _tile(m, 1024), 2048)
    o, s_hgrn = _hgrn_mix(proj, groups, s_hgrn, lb.reshape(1, -1), p["hgrn_norm_w"][j].reshape(1, -1), tables)
    x = _proj_ln([o], [p["w_out_c"][j].astype(BF16)], x, p["ln_c_g"][j], p["ln_c_b"][j], _tile(m, 1024))
    x = _moe_ln(x, p["moe_router"][j], j, p["moe_w1"], p["moe_w3"], p["moe_w2"],
                p["ln_moe_g"][j], p["ln_moe_b"][j])
    return x, s_hgrn


def _trunk(x, groups, s_rwkv, s_shift, s_gla, s_hgrn, lower_bounds, p):
    new_rwkv, new_shift, new_gla, new_hgrn = [], [], [], []
    tables = _chunk_tables()
    for layer in range(DEPTH):
        j = layer // 2
        if layer % 2 == 0:
            x, sh, sr, sg = _even_layer(x, j, groups, s_shift[j][:, None, :], s_rwkv[j], s_gla[j], tables, p)
            new_rwkv.append(sr)
            new_shift.append(sh)
            new_gla.append(sg)
        else:
            x, sc = _odd_layer(x, j, groups, lower_bounds[layer], s_hgrn[j], tables, p)
            new_hgrn.append(sc)
    return x, jnp.stack(new_rwkv), jnp.stack(new_shift), jnp.stack(new_gla), jnp.stack(new_hgrn)


def kernel(x_prompt, x_sample, state_rwkv, cache_rwkv_shift, state_gla, state_hgrn,
           w_in_ab, rwkv_mu, rwkv_w0, rwkv_w_up, rwkv_a0, rwkv_a_up, rwkv_g_up, rwkv_k_k, rwkv_k_a,
           rwkv_r_k, rwkv_gn_w, rwkv_gn_b, gla_gate_up, gla_gate_b, gla_norm_w, w_out_ab, ln_ab_g, ln_ab_b,
           ffn_w1, ffn_w3, ffn_w2, ln_ffn_g, ln_ffn_b,
           w_in_c, hgrn_lb, hgrn_norm_w, w_out_c, ln_c_g, ln_c_b,
           moe_router, moe_w1, moe_w3, moe_w2, ln_moe_g, ln_moe_b):
    p = dict(w_in_ab=w_in_ab, rwkv_mu=rwkv_mu, rwkv_w0=rwkv_w0, rwkv_w_up=rwkv_w_up, rwkv_a0=rwkv_a0,
             rwkv_a_up=rwkv_a_up, rwkv_g_up=rwkv_g_up, rwkv_k_k=rwkv_k_k, rwkv_k_a=rwkv_k_a,
             rwkv_r_k=rwkv_r_k, rwkv_gn_w=rwkv_gn_w, rwkv_gn_b=rwkv_gn_b, gla_gate_up=gla_gate_up,
             gla_gate_b=gla_gate_b, gla_norm_w=gla_norm_w, w_out_ab=w_out_ab, ln_ab_g=ln_ab_g,
             ln_ab_b=ln_ab_b, ffn_w1=ffn_w1, ffn_w3=ffn_w3, ffn_w2=ffn_w2, ln_ffn_g=ln_ffn_g,
             ln_ffn_b=ln_ffn_b, w_in_c=w_in_c, hgrn_norm_w=hgrn_norm_w, w_out_c=w_out_c, ln_c_g=ln_c_g,
             ln_c_b=ln_c_b, moe_router=moe_router, moe_w1=moe_w1, moe_w3=moe_w3, moe_w2=moe_w2,
             ln_moe_g=ln_moe_g, ln_moe_b=ln_moe_b)
    sm = jax.nn.softmax(hgrn_lb.astype(F32), axis=0)
    lower_bounds = jnp.cumsum(sm, axis=0) - sm[0]
    pb, pt, _ = x_prompt.shape
    sb, st, _ = x_sample.shape
    groups = [(pb, pt), (sb, st)]
    x = jnp.concatenate([_to_rows(x_prompt), _to_rows(x_sample)], axis=0)

    def with_prompt(state):
        return jnp.concatenate([jnp.zeros((state.shape[0], pb) + state.shape[2:], state.dtype), state], axis=1)

    y, s_rwkv, s_shift, s_gla, s_hgrn = _trunk(
        x, groups, with_prompt(state_rwkv), with_prompt(cache_rwkv_shift), with_prompt(state_gla),
        with_prompt(state_hgrn), lower_bounds, p)
    y_prompt = _from_rows(y[:pb * pt], pb, pt)
    y_sample = _from_rows(y[pb * pt:], sb, st)
    return (y_prompt, y_sample,
            s_rwkv[:, :pb], s_shift[:, :pb], s_gla[:, :pb], s_hgrn[:, :pb],
            s_rwkv[:, pb:], s_shift[:, pb:], s_gla[:, pb:], s_hgrn[:, pb:])
```

```python
import functools

import jax
import jax.numpy as jnp
import numpy as np
from jax import lax
from jax.experimental import pallas as pl
from jax.experimental.pallas import tpu as pltpu

F32 = jnp.float32
BF16 = jnp.bfloat16

D_MODEL = 1024
DEPTH = 4
CHUNK = 64

RWKV_HEADS = 8
RWKV_HD = 64
RWKV_DIM = RWKV_HEADS * RWKV_HD
DECAY_LORA = 64
AAA_LORA = 64
GATE_LORA = 128
RWKV_PROJ = 3 * RWKV_DIM + DECAY_LORA + AAA_LORA + GATE_LORA
RWKV_GN_EPS = RWKV_HD * 1e-5
DECAY_SCALE = 0.6065306597126334

GLA_HEADS = 4
GLA_DK = 64
GLA_DV = 128
GLA_KDIM = GLA_HEADS * GLA_DK
GLA_VDIM = GLA_HEADS * GLA_DV
GLA_RANK = 16
GLA_TAU = 16.0
GLA_GZW = 128

HGRN_HEADS = 8
HGRN_HD = 128
HGRN_DIM = HGRN_HEADS * HGRN_HD

D_FF = 2816
FF_CHUNK = 256
N_EXPERTS = 8
MOE_BLK = 512

ALPHA = (2 * DEPTH) ** 0.25
NORM_EPS = 1e-5

VMEM_LIMIT = 56 * 1024 * 1024
LANES = 128


def _cparams(*sem):
    return pltpu.CompilerParams(dimension_semantics=tuple(sem), vmem_limit_bytes=VMEM_LIMIT)


def _tile(m, pref):
    t = pref
    while m % t:
        t //= 2
    return t


def _sigmoid(x):
    return 1.0 / (1.0 + jnp.exp(-x))


def _layer_norm(y, g, b):
    mu = jnp.mean(y, axis=-1, keepdims=True)
    d = y - mu
    var = jnp.mean(d * d, axis=-1, keepdims=True)
    return d * lax.rsqrt(var + NORM_EPS) * g + b


def _dot(a, b):
    return jnp.dot(a.astype(BF16), b.astype(BF16), preferred_element_type=F32)


def _dot_nt(a, b):
    return lax.dot_general(a.astype(BF16), b.astype(BF16), (((1,), (1,)), ((), ())),
                           preferred_element_type=F32)


def _dot_tn(a, b):
    return lax.dot_general(a.astype(BF16), b.astype(BF16), (((0,), (0,)), ((), ())),
                           preferred_element_type=F32)


def _mm_kernel(x_ref, w_ref, o_ref):
    o_ref[...] = _dot(x_ref[...], w_ref[...]).astype(o_ref.dtype)


def _matmul(x, w, tm, tn, out_dtype=F32):
    m, k = x.shape
    n = w.shape[1]
    return pl.pallas_call(
        _mm_kernel,
        grid=(m // tm, n // tn),
        in_specs=[pl.BlockSpec((tm, k), lambda i, j: (i, 0)),
                  pl.BlockSpec((k, tn), lambda i, j: (0, j))],
        out_specs=pl.BlockSpec((tm, tn), lambda i, j: (i, j)),
        out_shape=jax.ShapeDtypeStruct((m, n), out_dtype),
        compiler_params=_cparams("parallel", "arbitrary"),
        name="matmul",
    )(x, w)


def _proj_ln_kernel(n_in, *refs):
    a_refs = refs[:n_in]
    w_refs = refs[n_in:2 * n_in]
    res_ref, g_ref, b_ref, o_ref = refs[2 * n_in:]
    acc = _dot(a_refs[0][...], w_refs[0][...])
    for a_ref, w_ref in zip(a_refs[1:], w_refs[1:]):
        acc = acc + _dot(a_ref[...], w_ref[...])
    o_ref[...] = _layer_norm(ALPHA * res_ref[...] + acc, g_ref[...], b_ref[...])


def _proj_ln(acts, ws, res, g, b, tm):
    m, d = res.shape
    n_in = len(acts)
    in_specs = [pl.BlockSpec((tm, a.shape[1]), lambda i: (i, 0)) for a in acts]
    in_specs += [pl.BlockSpec(w.shape, lambda i: (0, 0)) for w in ws]
    in_specs += [pl.BlockSpec((tm, d), lambda i: (i, 0)),
                 pl.BlockSpec((1, d), lambda i: (0, 0)),
                 pl.BlockSpec((1, d), lambda i: (0, 0))]
    return pl.pallas_call(
        functools.partial(_proj_ln_kernel, n_in),
        grid=(m // tm,),
        in_specs=in_specs,
        out_specs=pl.BlockSpec((tm, d), lambda i: (i, 0)),
        out_shape=jax.ShapeDtypeStruct((m, d), F32),
        compiler_params=_cparams("parallel"),
        name="proj_ln",
    )(*acts, *ws, res, g.reshape(1, d), b.reshape(1, d))


def _swiglu_acc(xb, w1_ref, w3_ref, w2_ref, acc_ref):
    acc_ref[...] = jnp.zeros_like(acc_ref)

    def chunk(c, carry):
        c0 = pl.multiple_of(c * FF_CHUNK, FF_CHUNK)
        h1 = jnp.dot(xb, w1_ref[:, pl.ds(c0, FF_CHUNK)].astype(BF16), preferred_element_type=F32)
        h3 = jnp.dot(xb, w3_ref[:, pl.ds(c0, FF_CHUNK)].astype(BF16), preferred_element_type=F32)
        hh = (h1 * _sigmoid(h1) * h3).astype(BF16)
        acc_ref[...] += jnp.dot(hh, w2_ref[pl.ds(c0, FF_CHUNK), :].astype(BF16), preferred_element_type=F32)
        return carry

    lax.fori_loop(0, D_FF // FF_CHUNK, chunk, 0)


def _ffn_ln_kernel(x_ref, w1_ref, w3_ref, w2_ref, g_ref, b_ref, o_ref, acc_ref):
    x = x_ref[...]
    _swiglu_acc(x.astype(BF16), w1_ref, w3_ref, w2_ref, acc_ref)
    o_ref[...] = _layer_norm(ALPHA * x + acc_ref[...], g_ref[...], b_ref[...])


def _ffn_ln(x, w1, w3, w2, g, b, tm):
    m, d = x.shape
    const = lambda i: (0, 0)
    return pl.pallas_call(
        _ffn_ln_kernel,
        grid=(m // tm,),
        in_specs=[pl.BlockSpec((tm, d), lambda i: (i, 0)),
                  pl.BlockSpec(w1.shape, const, pipeline_mode=pl.Buffered(1)),
                  pl.BlockSpec(w3.shape, const, pipeline_mode=pl.Buffered(1)),
                  pl.BlockSpec(w2.shape, const, pipeline_mode=pl.Buffered(1)),
                  pl.BlockSpec((1, d), const),
                  pl.BlockSpec((1, d), const)],
        out_specs=pl.BlockSpec((tm, d), lambda i: (i, 0)),
        out_shape=jax.ShapeDtypeStruct((m, d), F32),
        scratch_shapes=[pltpu.VMEM((tm, d), F32)],
        compiler_params=_cparams("parallel"),
        name="ffn_ln",
    )(x, w1, w3, w2, g.reshape(1, d), b.reshape(1, d))


def _to_rows(x):
    bsz, t_len, d = x.shape
    nb = t_len // CHUNK
    x = x.reshape(bsz // 2, 2, nb, CHUNK, d).transpose(0, 2, 1, 3, 4)
    return x.reshape(bsz * t_len, d)


def _from_rows(y, bsz, t_len):
    d = y.shape[-1]
    nb = t_len // CHUNK
    y = y.reshape(bsz // 2, nb, 2, CHUNK, d).transpose(0, 2, 1, 3, 4)
    return y.reshape(bsz, t_len, d)


def _seq_items(groups, per):
    blk, sidx, flags = [], [], []
    row0, s0 = 0, 0
    for bsz, t_len in groups:
        assert t_len % CHUNK == 0 and bsz % 2 == 0 and row0 % (2 * CHUNK) == 0
        nb = t_len // CHUNK
        for p in range(bsz // 2):
            for q in range(2 // per):
                for t in range(nb):
                    if per == 2:
                        blk.append(row0 // (2 * CHUNK) + p * nb + t)
                        sidx.append(s0 // 2 + p)
                    else:
                        blk.append(row0 // CHUNK + (p * nb + t) * 2 + q)
                        sidx.append(s0 + 2 * p + q)
                    flags.append(int(t == 0) + 2 * int(t == nb - 1))
        row0 += bsz * t_len
        s0 += bsz
    return tuple(jnp.asarray(np.asarray(a, np.int32)) for a in (blk, sidx, flags))


RW_G = 2 * RWKV_HEADS
RW_PAIRS = RW_G // 2
RW_UNROLL = 16


def _halves(x, lo):
    sa = jnp.sum(jnp.where(lo, x, 0.0), axis=1, keepdims=True)
    sb = jnp.sum(jnp.where(lo, 0.0, x), axis=1, keepdims=True)
    return jnp.where(lo, sa, sb)


def _rwkv_kernel(blk_ref, sidx_ref, flag_ref,
                 pr_ref, sh_ref, s0_ref,
                 mu_ref, w0_ref, wup_ref, a0_ref, aup_ref, gup_ref,
                 kk_ref, ka_ref, rk_ref, gnw_ref, gnb_ref, onehot_ref,
                 o_ref, st_ref, sho_ref, *scr):
    st = scr[:RW_PAIRS]
    prev_scr, ops_scr, post_scr, vt_scr, vb_scr, oblk_scr = scr[RW_PAIRS:]
    i = pl.program_id(0)
    flag = flag_ref[i]
    tt = CHUNK
    rows = 2 * tt
    hd = RWKV_HD

    @pl.when((flag & 1) == 1)
    def _():
        for q in range(2):
            for p in range(RWKV_HEADS // 2):
                st[q * 4 + p][...] = jnp.concatenate([s0_ref[q, 2 * p], s0_ref[q, 2 * p + 1]], axis=1)
            prev_scr[q] = sh_ref[q]

    rowi = lax.broadcasted_iota(jnp.int32, (rows, 1), 0)
    lo_rows = lax.broadcasted_iota(jnp.int32, (rows, LANES), 1) < hd

    x = pr_ref[...]
    xprev = jnp.where(rowi == 0, prev_scr[0],
                      jnp.where(rowi == tt, prev_scr[1], pltpu.roll(x, 1, 0)))
    prev_scr[0] = x[tt - 1:tt, :]
    prev_scr[1] = x[rows - 1:rows, :]
    xs = x + (xprev - x) * mu_ref[...]
    r = xs[:, 0:RWKV_DIM]
    k = xs[:, RWKV_DIM:2 * RWKV_DIM]
    v = xs[:, 2 * RWKV_DIM:3 * RWKV_DIM]
    lora = xs[:, 3 * RWKV_DIM:3 * RWKV_DIM + DECAY_LORA + AAA_LORA]
    gd = xs[:, 3 * RWKV_DIM + DECAY_LORA + AAA_LORA:]
    w = jnp.exp(-DECAY_SCALE * _sigmoid(w0_ref[...] + _dot(jnp.tanh(lora), wup_ref[...])))
    a = _sigmoid(a0_ref[...] + _dot(lora, aup_ref[...]))
    g = _dot(_sigmoid(gd), gup_ref[...])
    kkraw = k * kk_ref[...]
    kt = k * (1.0 + (a - 1.0) * ka_ref[...])
    prod = r * kt * rk_ref[...]
    bsums = []
    zpad = jnp.zeros((tt, LANES), F32)
    for p in range(RWKV_HEADS // 2):
        cs = slice(LANES * p, LANES * (p + 1))
        kk_p = kkraw[:, cs]
        kk_p = kk_p / jnp.maximum(jnp.sqrt(_halves(kk_p * kk_p, lo_rows)), 1e-12)
        kka_p = kk_p * a[:, cs]
        bsums.append(_halves(prod[:, cs], lo_rows))
        for q in range(2):
            pp = q * 4 + p
            rs = slice(q * tt, (q + 1) * tt)
            ops_scr[0, pp] = r[rs, cs]
            ops_scr[1, pp] = w[rs, cs]
            ops_scr[2, pp] = kt[rs, cs]
            ops_scr[3, pp] = kk_p[rs]
            ops_scr[4, pp] = kka_p[rs]
            vt = jnp.concatenate([v[rs, cs], zpad], axis=0).T[:, :tt]
            hi = vt.astype(BF16).astype(F32)
            low = vt - hi
            vt_scr[pp * hd:(pp + 1) * hd, :] = jnp.concatenate(
                [hi[:hd], low[:hd], hi[hd:], low[hd:]], axis=1).astype(BF16)
    post_scr[0] = g
    post_scr[1] = jnp.concatenate(bsums, axis=1) * v

    lo = lax.broadcasted_iota(jnp.int32, (hd, LANES), 1) < hd
    lo8 = lax.broadcasted_iota(jnp.int32, (8, LANES), 1) < hd
    row8 = lax.broadcasted_iota(jnp.int32, (8, LANES), 0)
    for c in range(tt // 8):
        cs = slice(8 * LANES * c, 8 * LANES * (c + 1))
        vb_scr[:, cs] = jnp.dot(vt_scr[...], onehot_ref[:, cs], preferred_element_type=F32)

    def step_group(tg, carry):
        t0 = tg * RW_UNROLL
        states = [st[pp][...] for pp in range(RW_PAIRS)]
        rounded = [s.astype(BF16) for s in states]
        for j in range(RW_UNROLL):
            t = t0 + j
            for pp in range(RW_PAIRS):
                s = states[pp]
                r_t = ops_scr[0, pp, pl.ds(t, 1), :]
                w_t = ops_scr[1, pp, pl.ds(t, 1), :]
                k_t = ops_scr[2, pp, pl.ds(t, 1), :]
                kk_t = ops_scr[3, pp, pl.ds(t, 1), :]
                kka_t = ops_scr[4, pp, pl.ds(t, 1), :]
                sa = _halves(s * kk_t, lo)
                v_t = vb_scr[pp * hd:(pp + 1) * hd, pl.ds(pl.multiple_of(t * LANES, LANES), LANES)]
                s = s * w_t - sa * kka_t + v_t * k_t
                states[pp] = s
                rounded[pp] = s.astype(BF16)
                r2 = jnp.where(row8 == 0, jnp.where(lo8, r_t, 0.0),
                               jnp.where(row8 == 1, jnp.where(lo8, 0.0, r_t), 0.0))
                o2 = _dot_nt(r2, rounded[pp])
                oblk_scr[2 * pp, pl.ds(t, 1), :] = o2[0:1]
                oblk_scr[2 * pp + 1, pl.ds(t, 1), :] = o2[1:2]
        for pp in range(RW_PAIRS):
            st[pp][...] = states[pp]
        return carry

    lax.fori_loop(0, tt // RW_UNROLL, step_group, 0)

    def group_norm(o):
        mu = jnp.mean(o, axis=1, keepdims=True)
        d = o - mu
        var = jnp.mean(d * d, axis=1, keepdims=True)
        return d * lax.rsqrt(var + RWKV_GN_EPS)

    on = jnp.concatenate(
        [jnp.concatenate([group_norm(oblk_scr[q * RWKV_HEADS + h]) for h in range(RWKV_HEADS)], axis=1)
         for q in range(2)], axis=0)
    out = (on * gnw_ref[...] + gnb_ref[...] + post_scr[1]) * post_scr[0]
    o_ref[...] = out.astype(o_ref.dtype)

    @pl.when((flag & 2) == 2)
    def _():
        for q in range(2):
            for p in range(RWKV_HEADS // 2):
                s = st[q * 4 + p][...]
                st_ref[q, 2 * p] = s[:, :hd]
                st_ref[q, 2 * p + 1] = s[:, hd:]
            sho_ref[q] = prev_scr[q]


def _value_onehot():
    r = np.arange(4 * CHUNK)[:, None]
    c = np.arange(CHUNK * LANES)[None, :]
    hit = ((r % CHUNK) == (c // LANES)) & ((r // (2 * CHUNK)) == ((c % LANES) // RWKV_HD))
    return jnp.asarray(hit.astype(np.float32), BF16)


def _rwkv_mix(proj, groups, shift0, s0, params):
    rows = proj.shape[0]
    n_seq = s0.shape[0]
    blk, sidx, flags = _seq_items(groups, 2)
    full = lambda a: pl.BlockSpec(a.shape, lambda i, b, s, f: (0,) * a.ndim)
    st_spec = pl.BlockSpec((2, RWKV_HEADS, RWKV_HD, RWKV_HD), lambda i, b, s, f: (s[i], 0, 0, 0))
    sh_spec = pl.BlockSpec((2, 1, RWKV_PROJ), lambda i, b, s, f: (s[i], 0, 0))
    grid_spec = pltpu.PrefetchScalarGridSpec(
        num_scalar_prefetch=3,
        grid=(flags.shape[0],),
        in_specs=[pl.BlockSpec((2 * CHUNK, RWKV_PROJ), lambda i, b, s, f: (b[i], 0)),
                  sh_spec, st_spec] + [full(a) for a in params]
                 + [pl.BlockSpec((4 * CHUNK, CHUNK * LANES), lambda i, b, s, f: (0, 0),
                                 pipeline_mode=pl.Buffered(1))],
        out_specs=[pl.BlockSpec((2 * CHUNK, RWKV_DIM), lambda i, b, s, f: (b[i], 0)), st_spec, sh_spec],
        scratch_shapes=[pltpu.VMEM((RWKV_HD, LANES), F32) for _ in range(RW_PAIRS)]
                       + [pltpu.VMEM((2, 1, RWKV_PROJ), F32),
                          pltpu.VMEM((5, RW_PAIRS, CHUNK, LANES), F32),
                          pltpu.VMEM((2, 2 * CHUNK, RWKV_DIM), F32),
                          pltpu.VMEM((RW_PAIRS * RWKV_HD, 4 * CHUNK), BF16),
                          pltpu.VMEM((RW_PAIRS * RWKV_HD, CHUNK * LANES), F32),
                          pltpu.VMEM((RW_G, CHUNK, RWKV_HD), F32)],
    )
    return pl.pallas_call(
        _rwkv_kernel,
        grid_spec=grid_spec,
        out_shape=[jax.ShapeDtypeStruct((rows, RWKV_DIM), BF16),
                   jax.ShapeDtypeStruct((n_seq, RWKV_HEADS, RWKV_HD, RWKV_HD), F32),
                   jax.ShapeDtypeStruct((n_seq, 1, RWKV_PROJ), F32)],
        compiler_params=_cparams("arbitrary"),
        name="rwkv_mix",
    )(blk, sidx, flags, proj, shift0, s0, *params, _value_onehot())


SUB = 16
LEVELS = (16, 8, 4, 2)
G_PIECES = 2
EX_B, EX_TAIL, EX_Q, EX_K1 = 0, 1, 2, 3
EX_LVL = EX_K1 + CHUNK // SUB - 1
EX_BLOCKS = EX_LVL + len(LEVELS)


def _chunk_tables():
    r = np.arange(CHUNK)[:, None]
    c = np.arange(CHUNK)[None, :]
    mats = [c <= r, c > r, (c >= (r // SUB) * SUB) & (c <= r)]
    for i in range(1, CHUNK // SUB):
        mats.append((c > r) & (c <= SUB * i - 1))
    masks = []
    for size in LEVELS:
        half = size // 2
        mid = (r // size) * size + half - 1
        upper = (r % size) >= half
        mats.append(np.where(upper, (c > mid) & (c <= r), (c > r) & (c <= mid)))
        masks.append(((r // size) == (c // size)) & upper & ((c % size) < half))
    e = np.concatenate(mats, axis=0).astype(np.float32)
    return (jnp.asarray(np.concatenate([e] * G_PIECES, axis=1), BF16),
            jnp.asarray(np.stack(masks).astype(np.float32)))


def _chunk_consts(dk):
    row = lax.broadcasted_iota(jnp.int32, (CHUNK, 1), 0)
    r = lax.broadcasted_iota(jnp.int32, (CHUNK, CHUNK), 0)
    c = lax.broadcasted_iota(jnp.int32, (CHUNK, CHUNK), 1)
    return dict(
        upper=[(row % size) >= size // 2 for size in LEVELS],
        blkrow=row // SUB,
        cross=(r // SUB) > (c // SUB),
        eye_t=r == c,
        eye=(lax.broadcasted_iota(jnp.int32, (dk, dk), 0)
             == lax.broadcasted_iota(jnp.int32, (dk, dk), 1)),
    )


def _split_bf16(x, pieces):
    out = []
    for _ in range(pieces - 1):
        p = x.astype(BF16)
        out.append(p)
        x = x - p.astype(F32)
    out.append(x.astype(BF16))
    return out


def _chunk_core(qs, ks, vs, gs, ss, ecat_ref, lmask_ref, cst):
    n = len(qs)
    ecat = ecat_ref[...]
    exs = [jnp.exp(jnp.dot(ecat, jnp.concatenate(_split_bf16(g, G_PIECES), axis=0),
                           preferred_element_type=F32)) for g in gs]
    blk = lambda ex, i: ex[i * CHUNK:(i + 1) * CHUNK]
    o_inter = [_dot(qs[h] * blk(exs[h], EX_B), ss[h]) for h in range(n)]
    p_cross = []
    for h in range(n):
        qg = qs[h] * blk(exs[h], EX_Q)
        rows = [jnp.zeros((SUB, CHUNK), F32)]
        for i in range(1, CHUNK // SUB):
            rows.append(_dot_nt(qg[SUB * i:SUB * (i + 1)], ks[h] * blk(exs[h], EX_K1 + i - 1)))
        p_cross.append(jnp.concatenate(rows, axis=0))
    p_diag = []
    for h in range(n):
        acc = jnp.zeros((CHUNK, CHUNK), F32)
        for lvl in range(len(LEVELS)):
            a = jnp.where(cst["upper"][lvl], qs[h], ks[h]) * blk(exs[h], EX_LVL + lvl)
            a_hi, a_lo = _split_bf16(a, 2)
            acc = acc + lmask_ref[lvl] * lax.dot_general(
                jnp.concatenate([a_hi, a_hi, a_lo], axis=1), jnp.concatenate([a_hi, a_lo, a_hi], axis=1),
                (((1,), (1,)), ((), ())), preferred_element_type=F32)
        p_diag.append(jnp.where(cst["eye_t"], jnp.sum(qs[h] * ks[h], axis=1, keepdims=True), acc))
    os_ = [o_inter[h] + _dot(jnp.where(cst["cross"], p_cross[h], p_diag[h]), vs[h]) for h in range(n)]
    s_new = []
    for h in range(n):
        dcol = jnp.sum(jnp.where(cst["eye"], exs[h][CHUNK - 1:CHUNK], 0.0), axis=1, keepdims=True)
        s_new.append(ss[h] * dcol + _dot_tn(ks[h] * blk(exs[h], EX_TAIL), vs[h]))
    return os_, s_new


def _rms_norm(o, w):
    return o * lax.rsqrt(jnp.mean(o * o, axis=-1, keepdims=True) + NORM_EPS) * w


def _log_sigmoid(x):
    return jnp.minimum(x, 0.0) - jnp.log1p(jnp.exp(-jnp.abs(x)))


GLA_DKP = 128
GLA_MAIN = 4 * GLA_HEADS * 128


def _gla_kernel(blk_ref, sidx_ref, flag_ref,
                gl_ref, gz_ref, s0_ref, gup_ref, gb_ref, nw_ref, ecat_ref, lmask_ref,
                o_ref, st_ref, *st):
    i = pl.program_id(0)
    flag = flag_ref[i]
    hw = GLA_HEADS * GLA_DKP

    @pl.when((flag & 1) == 1)
    def _():
        for q in range(2):
            for h in range(GLA_HEADS):
                st[q * GLA_HEADS + h][...] = s0_ref[q, h]

    cst = _chunk_consts(GLA_DKP)
    log_a = _log_sigmoid(_dot(gz_ref[...], gup_ref[...]) + gb_ref[...]) / GLA_TAU
    qs, ks, vs, gs = [], [], [], []
    for q in range(2):
        rs = slice(q * CHUNK, (q + 1) * CHUNK)
        for h in range(GLA_HEADS):
            qs.append(gl_ref[rs, GLA_DKP * h:GLA_DKP * (h + 1)] * GLA_DK ** -0.5)
            ks.append(gl_ref[rs, hw + GLA_DKP * h:hw + GLA_DKP * (h + 1)])
            vs.append(gl_ref[rs, 2 * hw + GLA_DV * h:2 * hw + GLA_DV * (h + 1)])
            gs.append(log_a[rs, GLA_DKP * h:GLA_DKP * (h + 1)])
    os_, s_new = _chunk_core(qs, ks, vs, gs, [r[...] for r in st], ecat_ref, lmask_ref, cst)
    for r, s in zip(st, s_new):
        r[...] = s
    normed = [_rms_norm(o, nw_ref[...]) for o in os_]
    on = jnp.concatenate([jnp.concatenate(normed[q * GLA_HEADS:(q + 1) * GLA_HEADS], axis=1)
                          for q in range(2)], axis=0)
    og = gl_ref[:, 3 * hw:3 * hw + GLA_VDIM]
    o_ref[...] = (on * (og * _sigmoid(og))).astype(o_ref.dtype)

    @pl.when((flag & 2) == 2)
    def _():
        for q in range(2):
            for h in range(GLA_HEADS):
                st_ref[q, h] = st[q * GLA_HEADS + h][...]


def _gla_mix(proj, groups, s0, gup, gb, nw, tables):
    rows = proj.shape[0]
    blk, sidx, flags = _seq_items(groups, 2)
    full = lambda a: pl.BlockSpec(a.shape, lambda i, b, s, f: (0,) * a.ndim)
    st_spec = pl.BlockSpec((2, GLA_HEADS, GLA_DKP, GLA_DV), lambda i, b, s, f: (s[i], 0, 0, 0))
    grid_spec = pltpu.PrefetchScalarGridSpec(
        num_scalar_prefetch=3,
        grid=(flags.shape[0],),
        in_specs=[pl.BlockSpec((2 * CHUNK, GLA_MAIN), lambda i, b, s, f: (b[i], 1)),
                  pl.BlockSpec((2 * CHUNK, GLA_GZW), lambda i, b, s, f: (b[i], RWKV_PROJ // GLA_GZW)),
                  st_spec, full(gup), full(gb), full(nw), full(tables[0]), full(tables[1])],
        out_specs=[pl.BlockSpec((2 * CHUNK, GLA_VDIM), lambda i, b, s, f: (b[i], 0)), st_spec],
        scratch_shapes=[pltpu.VMEM((GLA_DKP, GLA_DV), F32) for _ in range(2 * GLA_HEADS)],
    )
    return pl.pallas_call(
        _gla_kernel,
        grid_spec=grid_spec,
        out_shape=[jax.ShapeDtypeStruct((rows, GLA_VDIM), BF16),
                   jax.ShapeDtypeStruct(s0.shape, F32)],
        compiler_params=_cparams("arbitrary"),
        name="gla_mix",
    )(blk, sidx, flags, proj, proj, s0, gup, gb, nw, *tables)


def _hgrn_kernel(blk_ref, sidx_ref, flag_ref,
                 pc_ref, s0_ref, lb_ref, nw_ref, ecat_ref, lmask_ref,
                 o_ref, st_ref, *st):
    i = pl.program_id(0)
    flag = flag_ref[i]

    @pl.when((flag & 1) == 1)
    def _():
        for h in range(HGRN_HEADS):
            st[h][...] = s0_ref[0, h]

    cst = _chunk_consts(HGRN_HD)
    qs, ks, vs, gs = [], [], [], []
    for h in range(HGRN_HEADS):
        hs = slice(HGRN_HD * h, HGRN_HD * (h + 1))
        qx = pc_ref[:, HGRN_HD * h:HGRN_HD * (h + 1)]
        fx = pc_ref[:, HGRN_DIM + HGRN_HD * h:HGRN_DIM + HGRN_HD * (h + 1)]
        lb = lb_ref[:, hs]
        qs.append(qx * _sigmoid(qx))
        gs.append(jnp.log(lb + (1.0 - lb) * _sigmoid(fx)))
        ks.append((1.0 - lb) * _sigmoid(-fx))
        vs.append(pc_ref[:, 2 * HGRN_DIM + HGRN_HD * h:2 * HGRN_DIM + HGRN_HD * (h + 1)])
    os_, s_new = _chunk_core(qs, ks, vs, gs, [r[...] for r in st], ecat_ref, lmask_ref, cst)
    for r, s in zip(st, s_new):
        r[...] = s
    on = jnp.concatenate([_rms_norm(o, nw_ref[...]) for o in os_], axis=1)
    gx = pc_ref[:, 3 * HGRN_DIM:4 * HGRN_DIM]
    o_ref[...] = (on * (gx * _sigmoid(gx))).astype(o_ref.dtype)

    @pl.when((flag & 2) == 2)
    def _():
        for h in range(HGRN_HEADS):
            st_ref[0, h] = st[h][...]


def _hgrn_mix(proj, groups, s0, lb, nw, tables):
    rows = proj.shape[0]
    blk, sidx, flags = _seq_items(groups, 1)
    full = lambda a: pl.BlockSpec(a.shape, lambda i, b, s, f: (0,) * a.ndim)
    st_spec = pl.BlockSpec((1, HGRN_HEADS, HGRN_HD, HGRN_HD), lambda i, b, s, f: (s[i], 0, 0, 0))
    grid_spec = pltpu.PrefetchScalarGridSpec(
        num_scalar_prefetch=3,
        grid=(flags.shape[0],),
        in_specs=[pl.BlockSpec((CHUNK, 4 * HGRN_DIM), lambda i, b, s, f: (b[i], 0)),
                  st_spec, full(lb), full(nw), full(tables[0]), full(tables[1])],
        out_specs=[pl.BlockSpec((CHUNK, HGRN_DIM), lambda i, b, s, f: (b[i], 0)), st_spec],
        scratch_shapes=[pltpu.VMEM((HGRN_HD, HGRN_HD), F32) for _ in range(HGRN_HEADS)],
    )
    return pl.pallas_call(
        _hgrn_kernel,
        grid_spec=grid_spec,
        out_shape=[jax.ShapeDtypeStruct((rows, HGRN_DIM), BF16),
                   jax.ShapeDtypeStruct(s0.shape, F32)],
        compiler_params=_cparams("arbitrary"),
        name="hgrn_mix",
    )(blk, sidx, flags, proj, s0, lb, nw, *tables)


ROUTER_TM = 1024
MOE_TQ = 1024
DMA_UNROLL = 8
META_E1, META_E2, META_P1, META_P2, META_R1, META_R2 = range(6)


def _router_kernel(x_ref, wr_ref, meta_ref, cnt_ref, carry_scr):
    i = pl.program_id(0)

    @pl.when(i == 0)
    def _():
        carry_scr[...] = jnp.zeros_like(carry_scr)

    tm = x_ref.shape[0]
    logits = _dot(x_ref[...], wr_ref[...])
    lane = lax.broadcasted_iota(jnp.int32, (tm, LANES), 1)
    valid = lane < N_EXPERTS
    lg = jnp.where(valid, logits, -1e30)
    ex = jnp.where(valid, jnp.exp(lg - jnp.max(lg, axis=-1, keepdims=True)), 0.0)
    probs = ex / jnp.sum(ex, axis=-1, keepdims=True)
    p1 = jnp.max(probs, axis=-1, keepdims=True)
    i1 = jnp.min(jnp.where(jnp.logical_and(probs == p1, valid), lane, LANES), axis=-1, keepdims=True)
    rest = jnp.where(jnp.logical_or(lane == i1, jnp.logical_not(valid)), -1.0, probs)
    p2 = jnp.max(rest, axis=-1, keepdims=True)
    i2 = jnp.min(jnp.where(rest == p2, lane, LANES), axis=-1, keepdims=True)
    den = p1 + p2
    oh1 = (lane == i1).astype(F32)
    oh2 = (lane == i2).astype(F32)
    both = oh1 + oh2
    rr = lax.broadcasted_iota(jnp.int32, (tm, tm), 0)
    cc = lax.broadcasted_iota(jnp.int32, (tm, tm), 1)
    before = _dot((rr > cc).astype(F32), both) + carry_scr[...]
    r1 = jnp.sum(oh1 * before, axis=-1, keepdims=True)
    r2 = jnp.sum(oh2 * before, axis=-1, keepdims=True)
    carry_scr[...] += jnp.sum(both, axis=0, keepdims=True)
    meta = jnp.zeros((tm, LANES), F32)
    for idx, col in ((META_E1, i1.astype(F32)), (META_E2, i2.astype(F32)),
                     (META_P1, p1 / den), (META_P2, p2 / den), (META_R1, r1), (META_R2, r2)):
        meta = jnp.where(lane == idx, col, meta)
    meta_ref[...] = meta
    cnt_ref[...] = jnp.broadcast_to(carry_scr[...], cnt_ref.shape)


def _router(x, wr, tm):
    m, d = x.shape
    return pl.pallas_call(
        _router_kernel,
        grid=(m // tm,),
        in_specs=[pl.BlockSpec((tm, d), lambda i: (i, 0)),
                  pl.BlockSpec(wr.shape, lambda i: (0, 0))],
        out_specs=[pl.BlockSpec((tm, LANES), lambda i: (i, 0)),
                   pl.BlockSpec((8, LANES), lambda i: (0, 0))],
        out_shape=[jax.ShapeDtypeStruct((m, LANES), F32),
                   jax.ShapeDtypeStruct((8, LANES), F32)],
        scratch_shapes=[pltpu.VMEM((1, LANES), F32)],
        compiler_params=_cparams("arbitrary"),
        name="moe_router",
    )(x, wr)


def _scatter_kernel(dest_ref, x_ref, xs_in_ref, xs_ref, idx_smem, sem_idx, sem):
    del xs_in_ref
    i = pl.program_id(0)
    tq = x_ref.shape[0]
    cp = pltpu.make_async_copy(dest_ref.at[i], idx_smem, sem_idx)
    cp.start()
    cp.wait()

    def row_copy(t, slot):
        return pltpu.make_async_copy(x_ref.at[pl.ds(t, 1)],
                                     xs_ref.at[pl.ds(idx_smem[slot * tq + t], 1)], sem)

    def issue(t, carry):
        row_copy(t, 0).start()
        row_copy(t, 1).start()
        return carry

    def drain(t, carry):
        row_copy(t, 0).wait()
        row_copy(t, 1).wait()
        return carry

    lax.fori_loop(0, tq, issue, 0, unroll=DMA_UNROLL)
    lax.fori_loop(0, tq, drain, 0, unroll=DMA_UNROLL)


def _moe_scatter(x, dest, n_rows, tq):
    m, d = x.shape
    zeros = jnp.zeros((n_rows, d), x.dtype)
    any_spec = pl.BlockSpec(memory_space=pl.ANY)
    return pl.pallas_call(
        _scatter_kernel,
        grid=(m // tq,),
        in_specs=[any_spec, pl.BlockSpec((tq, d), lambda i: (i, 0)), any_spec],
        out_specs=any_spec,
        out_shape=jax.ShapeDtypeStruct((n_rows, d), x.dtype),
        scratch_shapes=[pltpu.SMEM((2 * tq,), jnp.int32),
                        pltpu.SemaphoreType.DMA(()),
                        pltpu.SemaphoreType.DMA(())],
        input_output_aliases={2: 0},
        compiler_params=pltpu.CompilerParams(dimension_semantics=("arbitrary",),
                                             has_side_effects=True),
        name="moe_scatter",
    )(dest, x, zeros)


def _expert_kernel(be_ref, bv_ref, x_ref, w1_ref, w3_ref, w2_ref, y_ref, acc_ref):
    b = pl.program_id(0)

    @pl.when(bv_ref[b] == 1)
    def _():
        _swiglu_acc(x_ref[...].astype(BF16), w1_ref, w3_ref, w2_ref, acc_ref)
        y_ref[...] = acc_ref[...]

    @pl.when(bv_ref[b] == 0)
    def _():
        y_ref[...] = jnp.zeros_like(y_ref)


def _moe_experts(xs, block_expert, block_valid, j, w1, w3, w2):
    rows, d = xs.shape
    wspec = lambda w: pl.BlockSpec((None, None) + w.shape[2:], lambda b, be, bv: (j, be[b], 0, 0),
                                   pipeline_mode=pl.Buffered(1))
    grid_spec = pltpu.PrefetchScalarGridSpec(
        num_scalar_prefetch=2,
        grid=(rows // MOE_BLK,),
        in_specs=[pl.BlockSpec((MOE_BLK, d), lambda b, be, bv: (b, 0)),
                  wspec(w1), wspec(w3), wspec(w2)],
        out_specs=pl.BlockSpec((MOE_BLK, d), lambda b, be, bv: (b, 0)),
        scratch_shapes=[pltpu.VMEM((MOE_BLK, d), F32)],
    )
    return pl.pallas_call(
        _expert_kernel,
        grid_spec=grid_spec,
        out_shape=jax.ShapeDtypeStruct((rows, d), F32),
        compiler_params=_cparams("arbitrary"),
        name="moe_experts",
    )(block_expert, block_valid, xs, w1, w3, w2)


def _combine_kernel(dest_ref, y_ref, x_ref, meta_ref, g_ref, b_ref, o_ref,
                    idx_smem, ya_scr, yb_scr, sem_idx, sem):
    i = pl.program_id(0)
    tq = x_ref.shape[0]
    cp = pltpu.make_async_copy(dest_ref.at[i], idx_smem, sem_idx)
    cp.start()
    cp.wait()

    def row_copy(t, slot, buf):
        return pltpu.make_async_copy(y_ref.at[pl.ds(idx_smem[slot * tq + t], 1)],
                                     buf.at[pl.ds(t, 1)], sem)

    def issue(t, carry):
        row_copy(t, 0, ya_scr).start()
        row_copy(t, 1, yb_scr).start()
        return carry

    def drain(t, carry):
        row_copy(t, 0, ya_scr).wait()
        row_copy(t, 1, yb_scr).wait()
        return carry

    lax.fori_loop(0, tq, issue, 0, unroll=DMA_UNROLL)
    lax.fori_loop(0, tq, drain, 0, unroll=DMA_UNROLL)
    meta = meta_ref[...]
    moe = ya_scr[...] * meta[:, META_P1:META_P1 + 1] + yb_scr[...] * meta[:, META_P2:META_P2 + 1]
    o_ref[...] = _layer_norm(ALPHA * x_ref[...] + moe, g_ref[...], b_ref[...])


def _moe_combine(dest, y, x, meta, g, b, tq):
    m, d = x.shape
    any_spec = pl.BlockSpec(memory_space=pl.ANY)
    const = lambda i: (0, 0)
    return pl.pallas_call(
        _combine_kernel,
        grid=(m // tq,),
        in_specs=[any_spec, any_spec,
                  pl.BlockSpec((tq, d), lambda i: (i, 0)),
                  pl.BlockSpec((tq, LANES), lambda i: (i, 0)),
                  pl.BlockSpec((1, d), const), pl.BlockSpec((1, d), const)],
        out_specs=pl.BlockSpec((tq, d), lambda i: (i, 0)),
        out_shape=jax.ShapeDtypeStruct((m, d), F32),
        scratch_shapes=[pltpu.SMEM((2 * tq,), jnp.int32),
                        pltpu.VMEM((tq, d), F32), pltpu.VMEM((tq, d), F32),
                        pltpu.SemaphoreType.DMA(()),
                        pltpu.SemaphoreType.DMA(())],
        compiler_params=_cparams("arbitrary"),
        name="moe_combine",
    )(dest, y, x, meta, g.reshape(1, d), b.reshape(1, d))


def _moe_ln(x, router, j, w1, w3, w2, g, b):
    m, d = x.shape
    tq = _tile(m, MOE_TQ)
    wr = jnp.pad(router, ((0, 0), (0, LANES - N_EXPERTS)))
    meta, cnt = _router(x, wr, _tile(m, ROUTER_TM))
    counts = cnt[0, :N_EXPERTS].astype(jnp.int32)
    padded = (counts + MOE_BLK - 1) // MOE_BLK * MOE_BLK
    pend = jnp.cumsum(padded)
    pstart = pend - padded
    n_blocks = (2 * m) // MOE_BLK + N_EXPERTS
    e1 = meta[:, META_E1].astype(jnp.int32)
    e2 = meta[:, META_E2].astype(jnp.int32)
    onehot = lambda e: (e[:, None] == jnp.arange(N_EXPERTS)[None, :]).astype(jnp.int32)
    dest1 = jnp.sum(onehot(e1) * pstart[None, :], axis=1) + meta[:, META_R1].astype(jnp.int32)
    dest2 = jnp.sum(onehot(e2) * pstart[None, :], axis=1) + meta[:, META_R2].astype(jnp.int32)
    dest = jnp.concatenate([dest1.reshape(m // tq, tq), dest2.reshape(m // tq, tq)], axis=1)
    block_start = jnp.arange(n_blocks, dtype=jnp.int32) * MOE_BLK
    block_expert = jnp.minimum(jnp.sum((block_start[:, None] >= pend[None, :]).astype(jnp.int32), axis=1),
                               N_EXPERTS - 1).astype(jnp.int32)
    block_valid = (block_start < pend[-1]).astype(jnp.int32)
    xs = _moe_scatter(x, dest, n_blocks * MOE_BLK, tq)
    y = _moe_experts(xs, block_expert, block_valid, j, w1, w3, w2)
    return _moe_combine(dest, y, x, meta, g, b, tq)


def _even_layer(x, j, groups, shift0, s_rwkv, s_gla, tables, p):
    m = x.shape[0]
    w_in = p["w_in_ab"][j]
    pg = w_in[:, RWKV_PROJ:]
    zcol = lambda n: jnp.zeros((D_MODEL, n), F32)

    def pad_heads(wq):
        wq = wq.reshape(D_MODEL, GLA_HEADS, GLA_DK)
        return jnp.pad(wq, ((0, 0), (0, 0), (0, GLA_DKP - GLA_DK))).reshape(D_MODEL, GLA_HEADS * GLA_DKP)

    gz0 = 2 * GLA_KDIM + GLA_VDIM
    packed = jnp.concatenate(
        [w_in[:, :RWKV_PROJ],
         pg[:, gz0:gz0 + GLA_RANK], zcol(2 * GLA_GZW - GLA_RANK),
         pad_heads(pg[:, 0:GLA_KDIM]), pad_heads(pg[:, GLA_KDIM:2 * GLA_KDIM]),
         pg[:, 2 * GLA_KDIM:gz0],
         pg[:, gz0 + GLA_RANK:]], axis=1).astype(BF16)
    proj = _matmul(x, packed, _tile(m, 1024), 2048)
    zl = jnp.zeros((DECAY_LORA, RWKV_DIM), F32)
    row = lambda a: a.reshape(1, -1)
    rw_params = [row(p["rwkv_mu"][j]), row(p["rwkv_w0"][j]),
                 jnp.concatenate([p["rwkv_w_up"][j], zl], axis=0).astype(BF16),
                 row(p["rwkv_a0"][j]),
                 jnp.concatenate([zl, p["rwkv_a_up"][j]], axis=0).astype(BF16),
                 p["rwkv_g_up"][j].astype(BF16),
                 row(p["rwkv_k_k"][j]), row(p["rwkv_k_a"][j]), row(p["rwkv_r_k"][j]),
                 row(p["rwkv_gn_w"][j]), row(p["rwkv_gn_b"][j])]
    o_rwkv, s_rwkv, shift_new = _rwkv_mix(proj, groups, shift0, s_rwkv, rw_params)
    pad_k = lambda a: jnp.pad(a.reshape(-1, GLA_HEADS, GLA_DK),
                              ((0, 0), (0, 0), (0, GLA_DKP - GLA_DK))).reshape(-1, GLA_HEADS * GLA_DKP)
    gup = jnp.pad(pad_k(p["gla_gate_up"][j]), ((0, GLA_GZW - GLA_RANK), (0, 0))).astype(BF16)
    s_gla_p = jnp.pad(s_gla, ((0, 0), (0, 0), (0, GLA_DKP - GLA_DK), (0, 0)))
    o_gla, s_gla_p = _gla_mix(proj, groups, s_gla_p, gup, pad_k(row(p["gla_gate_b"][j])),
                              row(p["gla_norm_w"][j]), tables)
    w_out = p["w_out_ab"][j].astype(BF16)
    x = _proj_ln([o_rwkv, o_gla], [w_out[:RWKV_DIM], w_out[RWKV_DIM:]], x,
                 p["ln_ab_g"][j], p["ln_ab_b"][j], _tile(m, 1024))
    x = _ffn_ln(x, p["ffn_w1"][j].astype(BF16), p["ffn_w3"][j].astype(BF16), p["ffn_w2"][j].astype(BF16),
                p["ln_ffn_g"][j], p["ln_ffn_b"][j], _tile(m, 1024))
    return x, shift_new[:, 0, :], s_rwkv, s_gla_p[:, :, :GLA_DK, :]


def _odd_layer(x, j, groups, lb, s_hgrn, tables, p):
    m = x.shape[0]
    proj = _matmul(x, p["w_in_c"][j].astype(BF16), _tile(m, 1024), 2048)
    o, s_hgrn = _hgrn_mix(proj, groups, s_hgrn, lb.reshape(1, -1), p["hgrn_norm_w"][j].reshape(1, -1), tables)
    x = _proj_ln([o], [p["w_out_c"][j].astype(BF16)], x, p["ln_c_g"][j], p["ln_c_b"][j], _tile(m, 1024))
    x = _moe_ln(x, p["moe_router"][j], j, p["moe_w1"], p["moe_w3"], p["moe_w2"],
                p["ln_moe_g"][j], p["ln_moe_b"][j])
    return x, s_hgrn


def _trunk(x, groups, s_rwkv, s_shift, s_gla, s_hgrn, lower_bounds, p):
    new_rwkv, new_shift, new_gla, new_hgrn = [], [], [], []
    tables = _chunk_tables()
    for layer in range(DEPTH):
        j = layer // 2
        if layer % 2 == 0:
            x, sh, sr, sg = _even_layer(x, j, groups, s_shift[j][:, None, :], s_rwkv[j], s_gla[j], tables, p)
            new_rwkv.append(sr)
            new_shift.append(sh)
            new_gla.append(sg)
        else:
            x, sc = _odd_layer(x, j, groups, lower_bounds[layer], s_hgrn[j], tables, p)
            new_hgrn.append(sc)
    return x, jnp.stack(new_rwkv), jnp.stack(new_shift), jnp.stack(new_gla), jnp.stack(new_hgrn)


def kernel(x_prompt, x_sample, state_rwkv, cache_rwkv_shift, state_gla, state_hgrn,
           w_in_ab, rwkv_mu, rwkv_w0, rwkv_w_up, rwkv_a0, rwkv_a_up, rwkv_g_up, rwkv_k_k, rwkv_k_a,
           rwkv_r_k, rwkv_gn_w, rwkv_gn_b, gla_gate_up, gla_gate_b, gla_norm_w, w_out_ab, ln_ab_g, ln_ab_b,
           ffn_w1, ffn_w3, ffn_w2, ln_ffn_g, ln_ffn_b,
           w_in_c, hgrn_lb, hgrn_norm_w, w_out_c, ln_c_g, ln_c_b,
           moe_router, moe_w1, moe_w3, moe_w2, ln_moe_g, ln_moe_b):
    p = dict(w_in_ab=w_in_ab, rwkv_mu=rwkv_mu, rwkv_w0=rwkv_w0, rwkv_w_up=rwkv_w_up, rwkv_a0=rwkv_a0,
             rwkv_a_up=rwkv_a_up, rwkv_g_up=rwkv_g_up, rwkv_k_k=rwkv_k_k, rwkv_k_a=rwkv_k_a,
             rwkv_r_k=rwkv_r_k, rwkv_gn_w=rwkv_gn_w, rwkv_gn_b=rwkv_gn_b, gla_gate_up=gla_gate_up,
             gla_gate_b=gla_gate_b, gla_norm_w=gla_norm_w, w_out_ab=w_out_ab, ln_ab_g=ln_ab_g,
             ln_ab_b=ln_ab_b, ffn_w1=ffn_w1, ffn_w3=ffn_w3, ffn_w2=ffn_w2, ln_ffn_g=ln_ffn_g,
             ln_ffn_b=ln_ffn_b, w_in_c=w_in_c, hgrn_norm_w=hgrn_norm_w, w_out_c=w_out_c, ln_c_g=ln_c_g,
             ln_c_b=ln_c_b, moe_router=moe_router, moe_w1=moe_w1, moe_w3=moe_w3, moe_w2=moe_w2,
             ln_moe_g=ln_moe_g, ln_moe_b=ln_moe_b)
    sm = jax.nn.softmax(hgrn_lb.astype(F32), axis=0)
    lower_bounds = jnp.cumsum(sm, axis=0) - sm[0]
    pb, pt, _ = x_prompt.shape
    sb, st, _ = x_sample.shape
    groups = [(pb, pt), (sb, st)]
    x = jnp.concatenate([_to_rows(x_prompt), _to_rows(x_sample)], axis=0)

    def with_prompt(state):
        return jnp.concatenate([jnp.zeros((state.shape[0], pb) + state.shape[2:], state.dtype), state], axis=1)

    y, s_rwkv, s_shift, s_gla, s_hgrn = _trunk(
        x, groups, with_prompt(state_rwkv), with_prompt(cache_rwkv_shift), with_prompt(state_gla),
        with_prompt(state_hgrn), lower_bounds, p)
    y_prompt = _from_rows(y[:pb * pt], pb, pt)
    y_sample = _from_rows(y[pb * pt:], sb, st)
    return (y_prompt, y_sample,
            s_rwkv[:, :pb], s_shift[:, :pb], s_gla[:, :pb], s_hgrn[:, :pb],
            s_rwkv[:, pb:], s_shift[:, pb:], s_gla[:, pb:], s_hgrn[:, pb:])
```
